```python
import jax, jax.numpy as jnp
from jax import lax
import numpy as np

D_MODEL = 1024
BATCH = 8
SEQ = 4096
DEPTH = 2

N_META = 16
CHUNK = 64
NORM_EPS = 1e-6
M_HEADS = 4
M_DH = D_MODEL // 2 // M_HEADS
M_W = M_HEADS * M_DH
GATE_CAP = 15.0
M_SLICE = 4 * M_W + 2 * M_HEADS
R_DH = 64
R_HEADS = D_MODEL // 2 // R_DH
R_W = R_HEADS * R_DH
R_RANK_W = 64
R_RANK_A = 64
R_RANK_G = 128
R_LN_EPS = 64e-5
R_SLICE = 3 * R_W + R_RANK_W + R_RANK_A + R_RANK_G
EVEN_IN = M_SLICE + R_SLICE
EVEN_MIX = M_W + R_W
T_HEADS = 4
T_DK = D_MODEL // T_HEADS
T_DV = 2 * T_DK
T_WV = T_HEADS * T_DV
ODD_IN = 2 * D_MODEL + 2 * T_WV
ROPE_BASE = 10000.0
D_FF = 2816
CONV_W = 3
N_EVEN = (DEPTH + 1) // 2
N_ODD = DEPTH // 2

kernel_name = 'hybrid_mlstm_rwkv7_retention_convffn'


def _rmsnorm(x, g):
    xf = x.astype(jnp.float32)
    y = xf * lax.rsqrt(jnp.mean(xf * xf, axis=-1, keepdims=True) + NORM_EPS)
    return (y * g.astype(jnp.float32)).astype(x.dtype)


def _head_layernorm(x, eps):
    mu = jnp.mean(x, axis=-1, keepdims=True)
    xc = x - mu
    return xc * lax.rsqrt(jnp.mean(xc * xc, axis=-1, keepdims=True) + eps)


def _softcap(x):
    return GATE_CAP * jnp.tanh(x / GATE_CAP)


def _shift(z):
    return jnp.pad(z, ((0, 0), (1, 0), (0, 0)))[:, :-1]


def _to_chunks(z, fill):
    b, l = z.shape[:2]
    pad = CHUNK - N_META
    z = jnp.pad(z, [(0, 0), (pad, 0)] + [(0, 0)] * (z.ndim - 2), constant_values=fill)
    nc = (l + pad) // CHUNK
    z = z.reshape((b, nc, CHUNK) + z.shape[2:])
    return z.transpose((1, 0, 3, 2) + tuple(range(4, z.ndim)))


def _from_chunks(y, l):
    nc, b, h, c, d = y.shape
    y = y.transpose(1, 0, 3, 2, 4).reshape(b, nc * c, h, d)
    return y[:, nc * c - l:]


def _mlstm(q, k, v, logi, logf):
    b, l, nh, dk = q.shape
    dv = v.shape[-1]
    xs = (_to_chunks(q, 0.0), _to_chunks(k, 0.0), _to_chunks(v, 0.0),
          _to_chunks(logi, -jnp.inf), _to_chunks(logf, 0.0))
    causal = jnp.tril(jnp.ones((CHUNK, CHUNK), dtype=bool))

    def step(carry, inp):
        c_st, n_st, m_st = carry
        qc, kc, vc, li, lf = inp
        bcum = jnp.cumsum(lf, axis=-1)
        g = bcum[..., -1]
        dmat = jnp.where(causal, bcum[..., :, None] - bcum[..., None, :] + li[..., None, :], -jnp.inf)
        inter = bcum + m_st[..., None]
        m_row = jnp.maximum(inter, jnp.max(dmat, axis=-1))
        s = jnp.einsum('bhtd,bhsd->bhts', qc, kc) * jnp.exp(dmat - m_row[..., None])
        w_inter = jnp.exp(inter - m_row)
        num = jnp.einsum('bhts,bhsv->bhtv', s, vc) + w_inter[..., None] * jnp.einsum('bhtd,bhdv->bhtv', qc, c_st)
        den = jnp.sum(s, axis=-1) + w_inter * jnp.einsum('bhtd,bhd->bht', qc, n_st)
        hc = num / jnp.maximum(jnp.abs(den), jnp.exp(-m_row))[..., None]
        a_log = g[..., None] - bcum + li
        m_new = jnp.maximum(g + m_st, jnp.max(a_log, axis=-1))
        carry_decay = jnp.exp(g + m_st - m_new)
        kw = kc * jnp.exp(a_log - m_new[..., None])[..., None]
        c_new = carry_decay[..., None, None] * c_st + jnp.einsum('bhsd,bhsv->bhdv', kw, vc)
        n_new = carry_decay[..., None] * n_st + jnp.sum(kw, axis=-2)
        return (c_new, n_new, m_new), hc

    init = (jnp.zeros((b, nh, dk, dv), jnp.float32), jnp.zeros((b, nh, dk), jnp.float32),
            jnp.zeros((b, nh), jnp.float32))
    _, hs = lax.scan(step, init, xs)
    return _from_chunks(hs, l)


def _rwkv7_scan(r, w, k, v, kk, a):
    b, l, nh, n = r.shape
    xs = tuple(t.transpose(1, 0, 2, 3) for t in (r, w, k, v, kk, a))

    def step(state, inp):
        rt, wt, kt, vt, kkt, at = inp
        sa = jnp.einsum('bhvk,bhk->bhv', state, -kkt)
        state = (state * wt[:, :, None, :] + sa[..., None] * (kkt * at)[:, :, None, :]
                 + vt[..., None] * kt[:, :, None, :])
        return state, jnp.einsum('bhvk,bhk->bhv', state, rt)

    _, out = lax.scan(step, jnp.zeros((b, nh, n, n), jnp.float32), xs)
    return out.transpose(1, 0, 2, 3)


def _rotary(x, pos):
    d = x.shape[-1]
    inv = 1.0 / (ROPE_BASE ** jnp.linspace(0.0, 1.0, d // 2, dtype=jnp.float32))
    ang = pos[:, None] * jnp.repeat(inv, 2)[None, :]
    sin, cos = jnp.sin(ang)[:, None, :], jnp.cos(ang)[:, None, :]
    rot = jnp.stack((-x[..., 1::2], x[..., ::2]), axis=-1).reshape(x.shape)
    return x * cos + rot * sin


def _retention(q, k, v):
    l = q.shape[1]
    log_gamma = jnp.log(1.0 - jnp.power(2.0, -5.0 - jnp.arange(T_HEADS, dtype=jnp.float32)))
    idx = jnp.arange(CHUNK, dtype=jnp.float32)
    diff = idx[:, None] - idx[None, :]
    intra = jnp.where(diff >= 0, jnp.exp(log_gamma[:, None, None] * jnp.maximum(diff, 0.0)), 0.0)
    q_dec = jnp.exp(log_gamma[:, None] * (idx + 1.0))[..., None]
    k_dec = jnp.exp(log_gamma[:, None] * (CHUNK - 1.0 - idx))[..., None]
    c_dec = jnp.exp(log_gamma * CHUNK)
    xs = (_to_chunks(q, 0.0), _to_chunks(k, 0.0), _to_chunks(v, 0.0))

    def step(state, inp):
        qc, kc, vc = inp
        s = jnp.einsum('bhtd,bhsd->bhts', qc, kc) * intra
        out = jnp.einsum('bhts,bhsv->bhtv', s, vc) + jnp.einsum('bhtd,bhdv->bhtv', qc * q_dec, state)
        state = c_dec[:, None, None] * state + jnp.einsum('bhsd,bhsv->bhdv', kc * k_dec, vc)
        return state, out

    b = q.shape[0]
    _, out = lax.scan(step, jnp.zeros((b, T_HEADS, T_DK, T_DV), jnp.float32), xs)
    return _from_chunks(out, l)


def _even_mixer(h, w_in, w_out, m_b_i, m_b_f, m_norm, r_mu, r_w0, r_w2, r_a0, r_a2, r_g2,
                r_k_k, r_k_a, r_r_k, r_ln_w, r_ln_b):
    b, l, _ = h.shape
    z = (h @ w_in).astype(jnp.float32)
    zm, zr = z[..., :M_SLICE], z[..., M_SLICE:]
    mq = zm[..., :M_W].reshape(b, l, M_HEADS, M_DH)
    mk = zm[..., M_W:2 * M_W].reshape(b, l, M_HEADS, M_DH) * (M_DH ** -0.5)
    mv = zm[..., 2 * M_W:3 * M_W].reshape(b, l, M_HEADS, M_DH)
    mo = jax.nn.sigmoid(zm[..., 3 * M_W:4 * M_W])
    logi = _softcap(zm[..., 4 * M_W:4 * M_W + M_HEADS] + m_b_i)
    logf = jax.nn.log_sigmoid(_softcap(zm[..., 4 * M_W + M_HEADS:] + m_b_f))
    hm = _mlstm(mq, mk, mv, logi, logf)
    hm = hm * lax.rsqrt(jnp.mean(hm * hm, axis=-1, keepdims=True) + NORM_EPS)
    hm = hm.reshape(b, l, M_W) * m_norm * mo
    zr = zr + (_shift(zr) - zr) * r_mu
    o1, o2, o3 = R_W, 2 * R_W, 3 * R_W
    o4, o5 = o3 + R_RANK_W, o3 + R_RANK_W + R_RANK_A
    rr, kr, vr = zr[..., :o1], zr[..., o1:o2], zr[..., o2:o3]
    xw, xa, xg = zr[..., o3:o4], zr[..., o4:o5], zr[..., o5:]
    wlog = -jax.nn.softplus(-(r_w0 + jnp.tanh(xw) @ r_w2)) - 0.5
    decay = jnp.exp(-jnp.exp(wlog))
    aa = jax.nn.sigmoid(r_a0 + xa @ r_a2)
    gg = jax.nn.sigmoid(xg) @ r_g2
    heads = lambda t: t.reshape(b, l, R_HEADS, R_DH)
    kk = heads(kr * r_k_k)
    kk = kk / jnp.maximum(jnp.sqrt(jnp.sum(kk * kk, axis=-1, keepdims=True)), 1e-12)
    kr = kr * (1.0 + (aa - 1.0) * r_k_a)
    rh, kh, vh = heads(rr), heads(kr), heads(vr)
    orr = _rwkv7_scan(rh, heads(decay), kh, vh, kk, heads(aa))
    orr = (_head_layernorm(orr, R_LN_EPS) * r_ln_w.reshape(R_HEADS, R_DH)
           + r_ln_b.reshape(R_HEADS, R_DH))
    orr = orr + jnp.sum(rh * kh * r_r_k, axis=-1, keepdims=True) * vh
    orr = orr.reshape(b, l, R_W) * gg
    return jnp.concatenate([hm, orr], axis=-1).astype(h.dtype) @ w_out


def _odd_mixer(h, w_in, w_out):
    b, l, _ = h.shape
    z = (h @ w_in).astype(jnp.float32)
    d = D_MODEL
    q = z[..., :d].reshape(b, l, T_HEADS, T_DK)
    k = z[..., d:2 * d].reshape(b, l, T_HEADS, T_DK) * (T_DK ** -0.5)
    v = z[..., 2 * d:2 * d + T_WV].reshape(b, l, T_HEADS, T_DV)
    gate = z[..., 2 * d + T_WV:]
    pos = jnp.arange(l, dtype=jnp.float32)
    o = _retention(_rotary(q, pos), _rotary(k, pos), v)
    o = _head_layernorm(o, NORM_EPS).reshape(b, l, T_WV) * jax.nn.silu(gate)
    return o.astype(h.dtype) @ w_out


def _conv_ffn(h, w_up, conv_w, conv_b, w_down):
    u = h @ w_up
    val, gate = u[..., :D_FF], u[..., D_FF:]
    gate = lax.conv_general_dilated(gate, conv_w[:, None, :], window_strides=(1,),
                                    padding=[(CONV_W - 1, 0)],
                                    dimension_numbers=('NWC', 'WIO', 'NWC'),
                                    feature_group_count=D_FF) + conv_b
    return (jax.nn.silu(gate) * val) @ w_down


def setup_inputs(seed: int = 0) -> dict:
    key = jax.random.key(seed)
    ks = list(jax.random.split(key, 32))
    def nrm(shape, scale):
        return jax.random.normal(ks.pop(), shape, jnp.float32) * scale
    d = D_MODEL
    x = nrm((BATCH, SEQ, d), 1.0)
    meta_tokens = nrm((N_META, d), 1.0)
    norm_mix = 1.0 + nrm((DEPTH, d), 0.02)
    norm_ffn = 1.0 + nrm((DEPTH, d), 0.02)
    norm_final = 1.0 + nrm((d,), 0.02)
    e_w_in = nrm((N_EVEN, d, EVEN_IN), d ** -0.5)
    e_w_out = nrm((N_EVEN, EVEN_MIX, d), EVEN_MIX ** -0.5)
    m_b_i = -1.0 + nrm((N_EVEN, M_HEADS), 0.1)
    m_b_f = jnp.linspace(3.0, 6.0, M_HEADS, dtype=jnp.float32)[None] + nrm((N_EVEN, M_HEADS), 0.1)
    m_norm = 1.0 + nrm((N_EVEN, M_W), 0.02)
    r_mu = jax.random.uniform(ks.pop(), (N_EVEN, R_SLICE), jnp.float32)
    r_w0 = (-6.5 + 5.0 * jnp.linspace(0.0, 1.0, R_W, dtype=jnp.float32) ** 0.85)[None] + nrm((N_EVEN, R_W), 0.1)
    r_w2 = nrm((N_EVEN, R_RANK_W, R_W), 0.1 * R_RANK_W ** -0.5)
    r_a0 = nrm((N_EVEN, R_W), 0.1)
    r_a2 = nrm((N_EVEN, R_RANK_A, R_W), 0.5 * R_RANK_A ** -0.5)
    r_g2 = nrm((N_EVEN, R_RANK_G, R_W), R_RANK_G ** -0.5)
    r_k_k = 0.85 + nrm((N_EVEN, R_W), 0.02)
    r_k_a = 1.0 + nrm((N_EVEN, R_W), 0.02)
    r_r_k = nrm((N_EVEN, R_HEADS, R_DH), 0.1)
    r_ln_w = 1.0 + nrm((N_EVEN, R_W), 0.02)
    r_ln_b = nrm((N_EVEN, R_W), 0.01)
    o_w_in = nrm((N_ODD, d, ODD_IN), d ** -0.5)
    o_w_out = nrm((N_ODD, T_WV, d), T_WV ** -0.5)
    f_w_up = nrm((DEPTH, d, 2 * D_FF), d ** -0.5)
    f_conv_w = nrm((DEPTH, CONV_W, D_FF), CONV_W ** -0.5)
    f_conv_b = nrm((DEPTH, D_FF), 0.01)
    f_w_down = nrm((DEPTH, D_FF, d), D_FF ** -0.5)
    return {'x': x, 'meta_tokens': meta_tokens, 'norm_mix': norm_mix, 'norm_ffn': norm_ffn,
            'norm_final': norm_final, 'e_w_in': e_w_in, 'e_w_out': e_w_out, 'm_b_i': m_b_i,
            'm_b_f': m_b_f, 'm_norm': m_norm, 'r_mu': r_mu, 'r_w0': r_w0, 'r_w2': r_w2,
            'r_a0': r_a0, 'r_a2': r_a2, 'r_g2': r_g2, 'r_k_k': r_k_k, 'r_k_a': r_k_a,
            'r_r_k': r_r_k, 'r_ln_w': r_ln_w, 'r_ln_b': r_ln_b, 'o_w_in': o_w_in,
            'o_w_out': o_w_out, 'f_w_up': f_w_up, 'f_conv_w': f_conv_w, 'f_conv_b': f_conv_b,
            'f_w_down': f_w_down}


def reference(x, meta_tokens, norm_mix, norm_ffn, norm_final, e_w_in, e_w_out, m_b_i, m_b_f,
              m_norm, r_mu, r_w0, r_w2, r_a0, r_a2, r_g2, r_k_k, r_k_a, r_r_k, r_ln_w, r_ln_b,
              o_w_in, o_w_out, f_w_up, f_conv_w, f_conv_b, f_w_down):
    b = x.shape[0]
    meta = jnp.broadcast_to(meta_tokens[None].astype(x.dtype), (b, N_META, D_MODEL))
    h = jnp.concatenate([meta, x], axis=1)
    for layer in range(DEPTH):
        j = layer // 2
        hn = _rmsnorm(h, norm_mix[layer])
        if layer % 2 == 0:
            h = h + _even_mixer(hn, e_w_in[j], e_w_out[j], m_b_i[j], m_b_f[j], m_norm[j],
                                r_mu[j], r_w0[j], r_w2[j], r_a0[j], r_a2[j], r_g2[j],
                                r_k_k[j], r_k_a[j], r_r_k[j], r_ln_w[j], r_ln_b[j])
        else:
            h = h + _odd_mixer(hn, o_w_in[j], o_w_out[j])
        h = h + _conv_ffn(_rmsnorm(h, norm_ffn[layer]), f_w_up[layer], f_conv_w[layer],
                          f_conv_b[layer], f_w_down[layer])
    return _rmsnorm(h, norm_final)[:, N_META:]
```

```python
import functools

import numpy as np
import jax
import jax.numpy as jnp
from jax import lax
from jax.experimental import pallas as pl
from jax.experimental.pallas import tpu as pltpu

F32 = jnp.float32
BF16 = jnp.bfloat16
HIGHEST = lax.Precision.HIGHEST

D_MODEL = 1024
N_META = 16
NORM_EPS = 1e-6
M_HEADS = 4
M_DH = 128
M_W = M_HEADS * M_DH
GATE_CAP = 15.0
R_DH = 64
R_HEADS = 8
R_W = R_HEADS * R_DH
R_PAIRS = R_HEADS // 2
R_LN_EPS = 64e-5
T_HEADS = 4
T_DK = 256
T_DV = 512
T_WV = T_HEADS * T_DV
ROPE_BASE = 10000.0
D_FF = 2816
FF_CHUNK = 256
N_FF_CHUNKS = D_FF // FF_CHUNK

LANES = 128
SUBLANES = 8
FRONT = 128
PAD = FRONT - N_META
CHUNK = 64
ROW_TILE = 512
SEQ_TILE = 384
VMEM_LIMIT = 56 * 1024 * 1024


def _const_spec(shape):
    nd = len(shape)
    return pl.BlockSpec(shape, lambda *_: (0,) * nd, pipeline_mode=pl.Buffered(1))


def _sigmoid(x):
    return 1.0 / (1.0 + jnp.exp(-x))


def _softplus(x):
    return jnp.maximum(x, 0.0) + jnp.log1p(jnp.exp(-jnp.abs(x)))


def _bdot(a, b):
    return jnp.dot(a.astype(BF16), b.astype(BF16), preferred_element_type=F32)


def _bdot_nt(a, b):
    return lax.dot_general(a.astype(BF16), b.astype(BF16), (((1,), (1,)), ((), ())),
                           preferred_element_type=F32)


def _bdot_tn(a, b):
    return lax.dot_general(a.astype(BF16), b.astype(BF16), (((0,), (0,)), ((), ())),
                           preferred_element_type=F32)


def _rmsnorm_bf16(x, g):
    ms = jnp.mean(x * x, axis=-1, keepdims=True)
    return (x * lax.rsqrt(ms + NORM_EPS) * g).astype(BF16)


def _col_chunks(n, width=512):
    return [(c, min(width, n - c)) for c in range(0, n, width)]


def _norm_proj_kernel(h_ref, g_ref, w_ref, o_ref, hn_ref):
    hn_ref[...] = _rmsnorm_bf16(h_ref[...], g_ref[...])
    for c0, cw in _col_chunks(w_ref.shape[1]):
        o_ref[:, c0:c0 + cw] = jnp.dot(hn_ref[...], w_ref[:, c0:c0 + cw],
                                       preferred_element_type=F32)


def _norm_proj_gates_kernel(h_ref, g_ref, w_ref, wgt_ref, o_ref, gt_ref, hn_ref):
    _norm_proj_kernel(h_ref, g_ref, w_ref, o_ref, hn_ref)
    gt_ref[...] = lax.dot_general(wgt_ref[...], hn_ref[...], (((1,), (1,)), ((), ())),
                                  preferred_element_type=F32)


def _norm_proj(h, g, w, wgt=None):
    m, d = h.shape
    n = w.shape[1]
    tm = ROW_TILE if n <= 4096 else ROW_TILE // 2
    in_specs = [pl.BlockSpec((tm, d), lambda i: (i, 0)), _const_spec((1, d)), _const_spec((d, n))]
    params = pltpu.CompilerParams(dimension_semantics=("arbitrary",), vmem_limit_bytes=VMEM_LIMIT)
    if wgt is None:
        return pl.pallas_call(
            _norm_proj_kernel, grid=(m // tm,), in_specs=in_specs,
            out_specs=pl.BlockSpec((tm, n), lambda i: (i, 0)),
            out_shape=jax.ShapeDtypeStruct((m, n), F32),
            scratch_shapes=[pltpu.VMEM((tm, d), BF16)],
            compiler_params=params, name="norm_proj")(h, g, w)
    return pl.pallas_call(
        _norm_proj_gates_kernel, grid=(m // tm,),
        in_specs=in_specs + [_const_spec(wgt.shape)],
        out_specs=[pl.BlockSpec((tm, n), lambda i: (i, 0)),
                   pl.BlockSpec((SUBLANES, tm), lambda i: (0, i))],
        out_shape=[jax.ShapeDtypeStruct((m, n), F32), jax.ShapeDtypeStruct((SUBLANES, m), F32)],
        scratch_shapes=[pltpu.VMEM((tm, d), BF16)],
        compiler_params=params, name="norm_proj_gates")(h, g, w, wgt)


def _gate_act(z, is_input_gate):
    capped = GATE_CAP * jnp.tanh(z / GATE_CAP)
    return jnp.where(is_input_gate, capped, -_softplus(-capped))


def _mlstm_kernel(q_ref, k_ref, v_ref, og_ref, gc_ref, gr_ref, bc_ref, br_ref, mn_ref,
                  out_ref, c_ref, n_ref, m_ref):
    t = pl.program_id(1)
    tt = q_ref.shape[0]

    @pl.when(t == 0)
    def _():
        c_ref[...] = jnp.zeros_like(c_ref)
        n_ref[...] = jnp.zeros_like(n_ref)
        m_ref[...] = jnp.zeros_like(m_ref)

    ri = lax.broadcasted_iota(jnp.int32, (CHUNK, CHUNK), 0)
    ci = lax.broadcasted_iota(jnp.int32, (CHUNK, CHUNK), 1)
    causal = ci <= ri
    lower = causal.astype(F32)
    upper = (ri <= ci).astype(F32)
    col_lane = lax.broadcasted_iota(jnp.int32, (1, LANES), 1)
    col_row = lax.broadcasted_iota(jnp.int32, (CHUNK, 1), 0)
    row_sub = lax.broadcasted_iota(jnp.int32, (SUBLANES, 1), 0)
    row_lane = lax.broadcasted_iota(jnp.int32, (1, CHUNK), 1)
    scale = M_DH ** -0.5

    def chunk(c, carry):
        r0 = pl.multiple_of(c * CHUNK, CHUNK)
        pos0 = t * tt + r0
        gc = _gate_act(gc_ref[pl.ds(r0, CHUNK), :] + bc_ref[...], col_lane < M_HEADS)
        gc = jnp.where(pos0 + col_row >= PAD, gc, jnp.where(col_lane < M_HEADS, -jnp.inf, 0.0))
        fcol = jnp.where(col_lane < M_HEADS, 0.0, gc)
        bcum_c = jnp.dot(lower, fcol, precision=HIGHEST, preferred_element_type=F32)
        gr = _gate_act(gr_ref[c] + br_ref[...], row_sub < M_HEADS)
        gr = jnp.where(pos0 + row_lane >= PAD, gr, jnp.where(row_sub < M_HEADS, -jnp.inf, 0.0))
        frow = jnp.where(row_sub < M_HEADS, 0.0, gr)
        bcum_r = jnp.dot(frow, upper, precision=HIGHEST, preferred_element_type=F32)

        for h in range(M_HEADS):
            hs = slice(h * M_DH, (h + 1) * M_DH)
            q = q_ref[pl.ds(r0, CHUNK), hs]
            k = k_ref[pl.ds(r0, CHUNK), hs] * scale
            v = v_ref[pl.ds(r0, CHUNK), hs]
            og = og_ref[pl.ds(r0, CHUNK), hs]
            bcol = bcum_c[:, M_HEADS + h:M_HEADS + h + 1]
            licol = gc[:, h:h + 1]
            brow = bcum_r[M_HEADS + h:M_HEADS + h + 1, :]
            lirow = gr[h:h + 1, :]
            c_st = c_ref[h]
            n_st = n_ref[h]
            m_st = m_ref[h][:, 0:1]

            g = brow[:, CHUNK - 1:CHUNK]
            dmat = jnp.where(causal, bcol - brow + lirow, -jnp.inf)
            inter = bcol + m_st
            m_row = jnp.maximum(inter, jnp.max(dmat, axis=-1, keepdims=True))
            qb = q.astype(BF16)
            vb = v.astype(BF16)
            s = _bdot_nt(qb, k) * jnp.exp(dmat - m_row)
            w_inter = jnp.exp(inter - m_row)
            num = _bdot(s, vb) + w_inter * _bdot(qb, c_st)
            den = (jnp.sum(s, axis=-1, keepdims=True)
                   + w_inter * jnp.sum(q * n_st, axis=-1, keepdims=True))
            hc = num / jnp.maximum(jnp.abs(den), jnp.exp(-m_row))

            a_row = g - brow + lirow
            m_new = jnp.maximum(g + m_st, jnp.max(a_row, axis=-1, keepdims=True))
            decay = jnp.exp(g + m_st - m_new)
            kw = k * jnp.exp(g - bcol + licol - m_new)
            c_ref[h] = decay * c_st + _bdot_tn(kw, vb)
            n_ref[h] = decay * n_st + jnp.sum(kw, axis=0, keepdims=True)
            m_ref[h] = jnp.broadcast_to(m_new, (1, LANES))

            hn = hc * lax.rsqrt(jnp.mean(hc * hc, axis=-1, keepdims=True) + NORM_EPS)
            out_ref[pl.ds(r0, CHUNK), hs] = (hn * mn_ref[:, hs] * _sigmoid(og)).astype(BF16)
        return carry

    lax.fori_loop(0, tt // CHUNK, chunk, 0)


def _mlstm(z, gt_chunks, bias_col, bias_row, m_norm, batch, lp):
    m = z.shape[0]
    tt = SEQ_TILE
    nt = lp // tt
    row = lambda b, t: b * nt + t
    gate_col_block = 4 * M_W // LANES + (3 * R_W + 256) // LANES
    in_specs = [
        pl.BlockSpec((tt, M_W), lambda b, t: (row(b, t), 0)),
        pl.BlockSpec((tt, M_W), lambda b, t: (row(b, t), 1)),
        pl.BlockSpec((tt, M_W), lambda b, t: (row(b, t), 2)),
        pl.BlockSpec((tt, M_W), lambda b, t: (row(b, t), 3)),
        pl.BlockSpec((tt, LANES), lambda b, t: (row(b, t), gate_col_block)),
        pl.BlockSpec((tt // CHUNK, SUBLANES, CHUNK), lambda b, t: (row(b, t), 0, 0)),
        _const_spec((1, LANES)), _const_spec((SUBLANES, CHUNK)), _const_spec((1, M_W)),
    ]
    return pl.pallas_call(
        _mlstm_kernel, grid=(batch, nt), in_specs=in_specs,
        out_specs=pl.BlockSpec((tt, M_W), lambda b, t: (row(b, t), 0)),
        out_shape=jax.ShapeDtypeStruct((m, M_W), BF16),
        scratch_shapes=[pltpu.VMEM((M_HEADS, M_DH, M_DH), F32),
                        pltpu.VMEM((M_HEADS, 1, M_DH), F32),
                        pltpu.VMEM((M_HEADS, 1, LANES), F32)],
        compiler_params=pltpu.CompilerParams(dimension_semantics=("arbitrary", "arbitrary"),
                                             vmem_limit_bytes=VMEM_LIMIT),
        name="mlstm")(z, z, z, z, z, gt_chunks, bias_col, bias_row, m_norm)


def _rwkv_kernel(r_ref, k_ref, v_ref, wa_ref, g_ref, mur_ref, muk_ref, muv_ref, muwa_ref, mug_ref,
                 w0_ref, w2_ref, a0_ref, a2_ref, g2_ref, kk_ref, ka_ref, rk_ref, lnw_ref, lnb_ref,
                 out_ref, cr_ref, ck_ref, cv_ref, cwa_ref, cg_ref, st_ref):
    t = pl.program_id(1)
    tt = r_ref.shape[0]
    two_c = 2 * CHUNK

    @pl.when(t == 0)
    def _():
        for ref in (cr_ref, ck_ref, cv_ref, cwa_ref, cg_ref, st_ref):
            ref[...] = jnp.zeros_like(ref)

    row_i = lax.broadcasted_iota(jnp.int32, (CHUNK, 1), 0)
    lo = lax.broadcasted_iota(jnp.int32, (1, LANES), 1) < R_DH
    ri = lax.broadcasted_iota(jnp.int32, (CHUNK, CHUNK), 0)
    ci = lax.broadcasted_iota(jnp.int32, (CHUNK, CHUNK), 1)
    lower = (ci <= ri).astype(F32)
    r2 = lax.broadcasted_iota(jnp.int32, (two_c, two_c), 0)
    c2 = lax.broadcasted_iota(jnp.int32, (two_c, two_c), 1)
    same_head = (r2 < CHUNK) == (c2 < CHUNK)
    tril_strict = same_head & (c2 < r2)
    tril_incl = same_head & (c2 <= r2)
    eye2 = (r2 == c2).astype(F32)

    def head_sum(x):
        parts = []
        for p in range(R_PAIRS):
            xp = x[:, p * LANES:(p + 1) * LANES]
            s0 = jnp.sum(jnp.where(lo, xp, 0.0), axis=-1, keepdims=True)
            s1 = jnp.sum(jnp.where(lo, 0.0, xp), axis=-1, keepdims=True)
            parts.append(jnp.where(lo, s0, s1))
        return jnp.concatenate(parts, axis=1)

    def stack_heads(x):
        return jnp.concatenate([jnp.where(lo, x, 0.0), jnp.where(lo, 0.0, x)], axis=0)

    def chunk(c, carry):
        r0 = pl.multiple_of(c * CHUNK, CHUNK)
        valid = (t * tt + r0 + row_i) >= PAD

        def shift_mix(x_ref, c_ref, mu_ref):
            x = x_ref[pl.ds(r0, CHUNK), :]
            prev = pltpu.roll(x, 1, axis=0)
            prev = jnp.where(row_i == 0, c_ref[SUBLANES - 1:SUBLANES, :], prev)
            c_ref[...] = x[CHUNK - SUBLANES:CHUNK, :]
            return jnp.where(valid, x + (prev - x) * mu_ref[...], 0.0)

        rr = shift_mix(r_ref, cr_ref, mur_ref)
        kr = shift_mix(k_ref, ck_ref, muk_ref)
        vr = shift_mix(v_ref, cv_ref, muv_ref)
        xwa = shift_mix(wa_ref, cwa_ref, muwa_ref)
        xg = shift_mix(g_ref, cg_ref, mug_ref)

        wlog = -_softplus(-(w0_ref[...] + _bdot(jnp.tanh(xwa), w2_ref[...]))) - 0.5
        lw = -jnp.exp(wlog)
        aa = _sigmoid(a0_ref[...] + _bdot(xwa, a2_ref[...]))
        gg = _bdot(_sigmoid(xg), g2_ref[...])
        kk = kr * kk_ref[...]
        kk = kk / jnp.maximum(jnp.sqrt(head_sum(kk * kk)), 1e-12)
        k2 = kr * (1.0 + (aa - 1.0) * ka_ref[...])

        outs = []
        for p in range(R_PAIRS):
            ps = slice(p * LANES, (p + 1) * LANES)
            cum = jnp.dot(lower, lw[:, ps], precision=HIGHEST, preferred_element_type=F32)
            e_pos = jnp.exp(cum)
            e_neg = jnp.exp(-cum)
            alpha = -kk[:, ps] * jnp.exp(cum - lw[:, ps])
            beta = kk[:, ps] * aa[:, ps] * e_neg
            kt = k2[:, ps] * e_neg
            rb = rr[:, ps] * e_pos
            p_end = e_pos[CHUNK - 1:CHUNK, :]

            al_st = stack_heads(alpha).astype(BF16)
            rb_st = stack_heads(rb).astype(BF16)
            v_st = stack_heads(vr[:, ps]).astype(BF16)
            be2 = jnp.concatenate([beta, beta], axis=0).astype(BF16)
            kt2 = jnp.concatenate([kt, kt], axis=0).astype(BF16)
            a_ab = jnp.where(tril_strict, _bdot_nt(al_st, be2), 0.0)
            a_ak = jnp.where(tril_strict, _bdot_nt(al_st, kt2), 0.0)
            w_rb = jnp.where(tril_incl, _bdot_nt(rb_st, be2), 0.0)
            w_rk = jnp.where(tril_incl, _bdot_nt(rb_st, kt2), 0.0)

            tinv = eye2 + a_ab
            apow = a_ab
            for _ in range(5):
                apow = _bdot(apow, apow)
                tinv = tinv + _bdot(tinv, apow)

            st = st_ref[p]
            stb = st.astype(BF16)
            u = _bdot(tinv, _bdot_nt(al_st, stb) + _bdot(a_ak, v_st))
            o_st = _bdot_nt(rb_st, stb) + _bdot(w_rb, u) + _bdot(w_rk, v_st)
            outs.append(o_st[:CHUNK] + o_st[CHUNK:])
            st_ref[p] = (p_end * st
                         + _bdot_tn(u, stack_heads(beta * p_end))
                         + _bdot_tn(v_st, stack_heads(kt * p_end)))
        o = jnp.concatenate(outs, axis=1)

        mean = head_sum(o) * (1.0 / R_DH)
        oc = o - mean
        var = head_sum(oc * oc) * (1.0 / R_DH)
        y = oc * lax.rsqrt(var + R_LN_EPS) * lnw_ref[...] + lnb_ref[...]
        y = y + head_sum(rr * k2 * rk_ref[...]) * vr
        out_ref[pl.ds(r0, CHUNK), :] = jnp.where(valid, y * gg, 0.0).astype(BF16)
        return carry

    lax.fori_loop(0, tt // CHUNK, chunk, 0)


def _rwkv(z, vecs, mats, batch, lp):
    m = z.shape[0]
    tt = SEQ_TILE
    nt = lp // tt
    row = lambda b, t: b * nt + t
    base = 4 * M_W
    in_specs = [
        pl.BlockSpec((tt, R_W), lambda b, t: (row(b, t), base // R_W)),
        pl.BlockSpec((tt, R_W), lambda b, t: (row(b, t), base // R_W + 1)),
        pl.BlockSpec((tt, R_W), lambda b, t: (row(b, t), base // R_W + 2)),
        pl.BlockSpec((tt, LANES), lambda b, t: (row(b, t), (base + 3 * R_W) // LANES)),
        pl.BlockSpec((tt, LANES), lambda b, t: (row(b, t), (base + 3 * R_W) // LANES + 1)),
    ]
    mu_r, mu_k, mu_v, mu_wa, mu_g, w0, a0, k_k, k_a, r_k, ln_w, ln_b = vecs
    w2, a2, g2 = mats
    args = [mu_r, mu_k, mu_v, mu_wa, mu_g, w0, w2, a0, a2, g2, k_k, k_a, r_k, ln_w, ln_b]
    in_specs += [_const_spec(a.shape) for a in args]
    return pl.pallas_call(
        _rwkv_kernel, grid=(batch, nt), in_specs=in_specs,
        out_specs=pl.BlockSpec((tt, R_W), lambda b, t: (row(b, t), 0)),
        out_shape=jax.ShapeDtypeStruct((m, R_W), BF16),
        scratch_shapes=[pltpu.VMEM((SUBLANES, R_W), F32), pltpu.VMEM((SUBLANES, R_W), F32),
                        pltpu.VMEM((SUBLANES, R_W), F32), pltpu.VMEM((SUBLANES, LANES), F32),
                        pltpu.VMEM((SUBLANES, LANES), F32),
                        pltpu.VMEM((R_PAIRS, LANES, LANES), F32)],
        compiler_params=pltpu.CompilerParams(dimension_semantics=("arbitrary", "arbitrary"),
                                             vmem_limit_bytes=VMEM_LIMIT),
        name="rwkv7")(z, z, z, z, z, *args)


def _retention_kernel(q_ref, k_ref, v_ref, gate_ref, cos_ref, sin_ref, out_ref, st_ref):
    t = pl.program_id(1)
    c = q_ref.shape[0]

    @pl.when(t == 0)
    def _():
        st_ref[...] = jnp.zeros_like(st_ref)

    ri = lax.broadcasted_iota(jnp.int32, (c, c), 0)
    ci = lax.broadcasted_iota(jnp.int32, (c, c), 1)
    diff = (ri - ci).astype(F32)
    idx = lax.broadcasted_iota(jnp.int32, (c, 1), 0).astype(F32)
    cos = cos_ref[...]
    sin = sin_ref[...]
    half = T_DK // 2

    def rotary(x):
        swapped = jnp.concatenate([x[:, half:], x[:, :half]], axis=1)
        return x * cos + swapped * sin

    for h in range(T_HEADS):
        log_gamma = float(np.log(1.0 - 2.0 ** (-5.0 - h)))
        ks = slice(h * T_DK, (h + 1) * T_DK)
        vs = slice(h * T_DV, (h + 1) * T_DV)
        q = rotary(q_ref[:, ks])
        k = rotary(k_ref[:, ks]) * (T_DK ** -0.5)
        vb = v_ref[:, vs].astype(BF16)
        intra = jnp.where(diff >= 0, jnp.exp(log_gamma * jnp.maximum(diff, 0.0)), 0.0)
        q_dec = jnp.exp(log_gamma * (idx + 1.0))
        k_dec = jnp.exp(log_gamma * (c - 1.0 - idx))
        st = st_ref[h]
        s = _bdot_nt(q, k) * intra
        o = _bdot(s, vb) + _bdot(q * q_dec, st)
        st_ref[h] = float(np.exp(log_gamma * c)) * st + _bdot_tn(k * k_dec, vb)

        mu = jnp.mean(o, axis=-1, keepdims=True)
        oc = o - mu
        var = jnp.mean(oc * oc, axis=-1, keepdims=True)
        gate = gate_ref[:, vs]
        out_ref[:, vs] = (oc * lax.rsqrt(var + NORM_EPS) * (gate * _sigmoid(gate))).astype(BF16)


def _retention(z, cos, sin, batch, lp):
    m = z.shape[0]
    tt = SEQ_TILE
    nt = lp // tt
    row = lambda b, t: b * nt + t
    in_specs = [
        pl.BlockSpec((tt, D_MODEL), lambda b, t: (row(b, t), 0)),
        pl.BlockSpec((tt, D_MODEL), lambda b, t: (row(b, t), 1)),
        pl.BlockSpec((tt, T_WV), lambda b, t: (row(b, t), 1)),
        pl.BlockSpec((tt, T_WV), lambda b, t: (row(b, t), 2)),
        pl.BlockSpec((tt, T_DK), lambda b, t: (t, 0)),
        pl.BlockSpec((tt, T_DK), lambda b, t: (t, 0)),
    ]
    return pl.pallas_call(
        _retention_kernel, grid=(batch, nt), in_specs=in_specs,
        out_specs=pl.BlockSpec((tt, T_WV), lambda b, t: (row(b, t), 0)),
        out_shape=jax.ShapeDtypeStruct((m, T_WV), BF16),
        scratch_shapes=[pltpu.VMEM((T_HEADS, T_DK, T_DV), F32)],
        compiler_params=pltpu.CompilerParams(dimension_semantics=("arbitrary", "arbitrary"),
                                             vmem_limit_bytes=VMEM_LIMIT),
        name="retention")(z, z, z, z, cos, sin)


def _out_proj_kernel(n_in, *refs):
    h_ref = refs[0]
    a_refs = refs[1:1 + n_in]
    w_refs = refs[1 + n_in:1 + 2 * n_in]
    o_ref = refs[1 + 2 * n_in]
    acc = h_ref[...]
    for a_ref, w_ref in zip(a_refs, w_refs):
        acc = acc + jnp.dot(a_ref[...], w_ref[...], preferred_element_type=F32)
    o_ref[...] = acc


def _out_proj(h, acts, weights):
    m, d = h.shape
    tm = ROW_TILE
    in_specs = [pl.BlockSpec((tm, d), lambda i: (i, 0))]
    in_specs += [pl.BlockSpec((tm, a.shape[1]), lambda i: (i, 0)) for a in acts]
    in_specs += [_const_spec(w.shape) for w in weights]
    return pl.pallas_call(
        functools.partial(_out_proj_kernel, len(acts)), grid=(m // tm,), in_specs=in_specs,
        out_specs=pl.BlockSpec((tm, d), lambda i: (i, 0)),
        out_shape=jax.ShapeDtypeStruct((m, d), F32),
        compiler_params=pltpu.CompilerParams(dimension_semantics=("arbitrary",),
                                             vmem_limit_bytes=VMEM_LIMIT),
        name="out_proj")(h, *acts, *weights)


def _ffn_kernel(h_ref, g_ref, wv_ref, wg_ref, cw_ref, cb_ref, wd_ref, o_ref, carry_ref, hn_ref):
    tm = h_ref.shape[0]

    @pl.when(pl.program_id(0) == 0)
    def _():
        carry_ref[...] = jnp.zeros_like(carry_ref)

    h = h_ref[...]
    hn_ref[...] = _rmsnorm_bf16(h, g_ref[...])
    o_ref[...] = h
    row = lax.broadcasted_iota(jnp.int32, (tm, 1), 0)

    def chunk(c, carry):
        hn = hn_ref[...]
        val = jnp.dot(hn, wv_ref[c], preferred_element_type=F32)
        gate = jnp.dot(hn, wg_ref[c], preferred_element_type=F32)
        tail = carry_ref[c]
        prev1 = jnp.where(row == 0, tail[SUBLANES - 1:SUBLANES, :], pltpu.roll(gate, 1, axis=0))
        prev2 = pltpu.roll(gate, 2, axis=0)
        prev2 = jnp.where(row == 0, tail[SUBLANES - 2:SUBLANES - 1, :], prev2)
        prev2 = jnp.where(row == 1, tail[SUBLANES - 1:SUBLANES, :], prev2)
        carry_ref[c] = gate[tm - SUBLANES:tm, :]
        cw = cw_ref[c]
        conv = cw[2:3, :] * gate + cw[1:2, :] * prev1 + cw[0:1, :] * prev2 + cb_ref[c]
        act = (conv * _sigmoid(conv) * val).astype(BF16)
        o_ref[...] += jnp.dot(act, wd_ref[c], preferred_element_type=F32)
        return carry

    lax.fori_loop(0, N_FF_CHUNKS, chunk, 0)


def _ffn(h, g, wv, wg, cw, cb, wd):
    m, d = h.shape
    tm = ROW_TILE
    in_specs = [pl.BlockSpec((tm, d), lambda i: (i, 0)), _const_spec((1, d))]
    in_specs += [_const_spec(a.shape) for a in (wv, wg, cw, cb, wd)]
    return pl.pallas_call(
        _ffn_kernel, grid=(m // tm,), in_specs=in_specs,
        out_specs=pl.BlockSpec((tm, d), lambda i: (i, 0)),
        out_shape=jax.ShapeDtypeStruct((m, d), F32),
        scratch_shapes=[pltpu.VMEM((N_FF_CHUNKS, SUBLANES, FF_CHUNK), F32),
                        pltpu.VMEM((tm, d), BF16)],
        compiler_params=pltpu.CompilerParams(dimension_semantics=("arbitrary",),
                                             vmem_limit_bytes=VMEM_LIMIT),
        name="conv_ffn")(h, g, wv, wg, cw, cb, wd)


def _final_norm_kernel(h_ref, g_ref, o_ref):
    x = h_ref[...]
    ms = jnp.mean(x * x, axis=-1, keepdims=True)
    o_ref[0] = x * lax.rsqrt(ms + NORM_EPS) * g_ref[...]


def _final_norm(h, g, batch, lp):
    d = h.shape[1]
    per_batch = lp // FRONT
    seq = lp - FRONT
    return pl.pallas_call(
        _final_norm_kernel, grid=(batch, per_batch - 1),
        in_specs=[pl.BlockSpec((FRONT, d), lambda b, j: (b * per_batch + j + 1, 0)),
                  _const_spec((1, d))],
        out_specs=pl.BlockSpec((1, FRONT, d), lambda b, j: (b, j, 0)),
        out_shape=jax.ShapeDtypeStruct((batch, seq, d), F32),
        compiler_params=pltpu.CompilerParams(dimension_semantics=("arbitrary", "arbitrary")),
        name="final_norm")(h, g)


def _ffn_weights(w_up, conv_w, conv_b, w_down):
    d = w_up.shape[0]
    split = lambda w: w.reshape(d, N_FF_CHUNKS, FF_CHUNK).transpose(1, 0, 2).astype(BF16)
    wv = split(w_up[:, :D_FF])
    wg = split(w_up[:, D_FF:])
    cw = jnp.pad(conv_w, ((0, SUBLANES - conv_w.shape[0]), (0, 0)))
    cw = cw.reshape(SUBLANES, N_FF_CHUNKS, FF_CHUNK).transpose(1, 0, 2)
    cb = conv_b.reshape(N_FF_CHUNKS, 1, FF_CHUNK)
    wd = w_down.reshape(N_FF_CHUNKS, FF_CHUNK, d).astype(BF16)
    return wv, wg, cw, cb, wd


def kernel(x, meta_tokens, norm_mix, norm_ffn, norm_final, e_w_in, e_w_out, m_b_i, m_b_f, m_norm,
           r_mu, r_w0, r_w2, r_a0, r_a2, r_g2, r_k_k, r_k_a, r_r_k, r_ln_w, r_ln_b, o_w_in, o_w_out,
           f_w_up, f_conv_w, f_conv_b, f_w_down):
    batch, seq, d = x.shape
    lp = seq + FRONT
    assert d == D_MODEL and lp % SEQ_TILE == 0 and (batch * lp) % ROW_TILE == 0
    m = batch * lp
    row = lambda v: v.reshape(1, -1).astype(F32)

    meta = jnp.broadcast_to(meta_tokens[None].astype(x.dtype), (batch, N_META, d))
    h = jnp.concatenate([jnp.zeros((batch, PAD, d), x.dtype), meta, x], axis=1).reshape(m, d)

    w_in = e_w_in[0]
    n_m = 4 * M_W
    gates_w = w_in[:, n_m:n_m + 2 * M_HEADS]
    w0 = jnp.concatenate([w_in[:, :n_m], w_in[:, n_m + 2 * M_HEADS:], gates_w,
                          jnp.zeros((d, LANES - 2 * M_HEADS), F32)], axis=1).astype(BF16)
    z, gt = _norm_proj(h, row(norm_mix[0]), w0, gates_w.T.astype(BF16))
    gt_chunks = gt.reshape(SUBLANES, m // CHUNK, CHUNK).transpose(1, 0, 2)
    gate_bias = jnp.concatenate([m_b_i[0], m_b_f[0]])
    bias_col = jnp.pad(gate_bias, (0, LANES - 2 * M_HEADS)).reshape(1, LANES)
    bias_row = jnp.broadcast_to(gate_bias[:, None], (SUBLANES, CHUNK))
    mix_m = _mlstm(z, gt_chunks, bias_col, bias_row, row(m_norm[0]), batch, lp)

    mu = r_mu[0]
    rank_wa = r_w2.shape[1] + r_a2.shape[1]
    vecs = [row(mu[:R_W]), row(mu[R_W:2 * R_W]), row(mu[2 * R_W:3 * R_W]),
            row(mu[3 * R_W:3 * R_W + rank_wa]), row(mu[3 * R_W + rank_wa:]),
            row(r_w0[0]), row(r_a0[0]), row(r_k_k[0]), row(r_k_a[0]), row(r_r_k[0]),
            row(r_ln_w[0]), row(r_ln_b[0])]
    w2 = jnp.concatenate([r_w2[0], jnp.zeros_like(r_a2[0])], axis=0).astype(BF16)
    a2 = jnp.concatenate([jnp.zeros_like(r_w2[0]), r_a2[0]], axis=0).astype(BF16)
    mix_r = _rwkv(z, vecs, [w2, a2, r_g2[0].astype(BF16)], batch, lp)

    w_out = e_w_out[0].astype(BF16)
    h = _out_proj(h, [mix_m, mix_r], [w_out[:M_W], w_out[M_W:]])
    h = _ffn(h, row(norm_ffn[0]), *_ffn_weights(f_w_up[0], f_conv_w[0], f_conv_b[0], f_w_down[0]))

    w_in = o_w_in[0]
    perm = np.concatenate([np.arange(0, T_DK, 2), np.arange(1, T_DK, 2)])
    qk_cols = np.concatenate([hh * T_DK + perm for hh in range(T_HEADS)])
    w1 = jnp.concatenate([w_in[:, qk_cols], w_in[:, D_MODEL + qk_cols], w_in[:, 2 * D_MODEL:]],
                         axis=1).astype(BF16)
    z = _norm_proj(h, row(norm_mix[1]), w1)
    inv = 1.0 / (ROPE_BASE ** jnp.linspace(0.0, 1.0, T_DK // 2, dtype=F32))
    pos = jnp.arange(lp, dtype=F32) - PAD
    ang = pos[:, None] * inv[None, :]
    cos = jnp.concatenate([jnp.cos(ang), jnp.cos(ang)], axis=1)
    sin = jnp.concatenate([-jnp.sin(ang), jnp.sin(ang)], axis=1)
    o = _retention(z, cos, sin, batch, lp)
    h = _out_proj(h, [o], [o_w_out[0].astype(BF16)])
    h = _ffn(h, row(norm_ffn[1]), *_ffn_weights(f_w_up[1], f_conv_w[1], f_conv_b[1], f_w_down[1]))

    return _final_norm(h, row(norm_final), batch, lp)
```

```python
import functools

import numpy as np
import jax
import jax.numpy as jnp
from jax import lax
from jax.experimental import pallas as pl
from jax.experimental.pallas import tpu as pltpu

F32 = jnp.float32
BF16 = jnp.bfloat16
HIGHEST = lax.Precision.HIGHEST

D_MODEL = 1024
N_META = 16
NORM_EPS = 1e-6
M_HEADS = 4
M_DH = 128
M_W = M_HEADS * M_DH
GATE_CAP = 15.0
R_DH = 64
R_HEADS = 8
R_W = R_HEADS * R_DH
R_PAIRS = R_HEADS // 2
R_LN_EPS = 64e-5
T_HEADS = 4
T_DK = 256
T_DV = 512
T_WV = T_HEADS * T_DV
ROPE_BASE = 10000.0
D_FF = 2816
FF_CHUNK = 256
N_FF_CHUNKS = D_FF // FF_CHUNK

LANES = 128
SUBLANES = 8
FRONT = 128
PAD = FRONT - N_META
CHUNK = 64
ROW_TILE = 512
SEQ_TILE = 384
VMEM_LIMIT = 56 * 1024 * 1024


def _const_spec(shape):
    nd = len(shape)
    return pl.BlockSpec(shape, lambda *_: (0,) * nd, pipeline_mode=pl.Buffered(1))


def _sigmoid(x):
    return 1.0 / (1.0 + jnp.exp(-x))


def _softplus(x):
    return jnp.maximum(x, 0.0) + jnp.log1p(jnp.exp(-jnp.abs(x)))


def _bdot(a, b):
    return jnp.dot(a.astype(BF16), b.astype(BF16), preferred_element_type=F32)


def _bdot_nt(a, b):
    return lax.dot_general(a.astype(BF16), b.astype(BF16), (((1,), (1,)), ((), ())),
                           preferred_element_type=F32)


def _bdot_tn(a, b):
    return lax.dot_general(a.astype(BF16), b.astype(BF16), (((0,), (0,)), ((), ())),
                           preferred_element_type=F32)


def _rmsnorm_bf16(x, g):
    ms = jnp.mean(x * x, axis=-1, keepdims=True)
    return (x * lax.rsqrt(ms + NORM_EPS) * g).astype(BF16)


def _col_chunks(n, width=512):
    return [(c, min(width, n - c)) for c in range(0, n, width)]


def _norm_proj_kernel(h_ref, g_ref, w_ref, o_ref, hn_ref):
    hn_ref[...] = _rmsnorm_bf16(h_ref[...], g_ref[...])
    for c0, cw in _col_chunks(w_ref.shape[1]):
        o_ref[:, c0:c0 + cw] = jnp.dot(hn_ref[...], w_ref[:, c0:c0 + cw],
                                       preferred_element_type=F32)


def _norm_proj_gates_kernel(h_ref, g_ref, w_ref, wgt_ref, o_ref, gt_ref, hn_ref):
    _norm_proj_kernel(h_ref, g_ref, w_ref, o_ref, hn_ref)
    gt_ref[...] = lax.dot_general(wgt_ref[...], hn_ref[...], (((1,), (1,)), ((), ())),
                                  preferred_element_type=F32)


def _norm_proj(h, g, w, wgt=None):
    m, d = h.shape
    n = w.shape[1]
    tm = ROW_TILE if n <= 4096 else ROW_TILE // 2
    in_specs = [pl.BlockSpec((tm, d), lambda i: (i, 0)), _const_spec((1, d)), _const_spec((d, n))]
    params = pltpu.CompilerParams(dimension_semantics=("arbitrary",), vmem_limit_bytes=VMEM_LIMIT)
    if wgt is None:
        return pl.pallas_call(
            _norm_proj_kernel, grid=(m // tm,), in_specs=in_specs,
            out_specs=pl.BlockSpec((tm, n), lambda i: (i, 0)),
            out_shape=jax.ShapeDtypeStruct((m, n), F32),
            scratch_shapes=[pltpu.VMEM((tm, d), BF16)],
            compiler_params=params, name="norm_proj")(h, g, w)
    return pl.pallas_call(
        _norm_proj_gates_kernel, grid=(m // tm,),
        in_specs=in_specs + [_const_spec(wgt.shape)],
        out_specs=[pl.BlockSpec((tm, n), lambda i: (i, 0)),
                   pl.BlockSpec((SUBLANES, tm), lambda i: (0, i))],
        out_shape=[jax.ShapeDtypeStruct((m, n), F32), jax.ShapeDtypeStruct((SUBLANES, m), F32)],
        scratch_shapes=[pltpu.VMEM((tm, d), BF16)],
        compiler_params=params, name="norm_proj_gates")(h, g, w, wgt)


def _gate_act(z, is_input_gate):
    capped = GATE_CAP * jnp.tanh(z / GATE_CAP)
    return jnp.where(is_input_gate, capped, -_softplus(-capped))


def _mlstm_kernel(q_ref, k_ref, v_ref, og_ref, gc_ref, gr_ref, bc_ref, br_ref, mn_ref,
                  out_ref, c_ref, n_ref, m_ref):
    t = pl.program_id(1)
    tt = q_ref.shape[0]

    @pl.when(t == 0)
    def _():
        c_ref[...] = jnp.zeros_like(c_ref)
        n_ref[...] = jnp.zeros_like(n_ref)
        m_ref[...] = jnp.zeros_like(m_ref)

    ri = lax.broadcasted_iota(jnp.int32, (CHUNK, CHUNK), 0)
    ci = lax.broadcasted_iota(jnp.int32, (CHUNK, CHUNK), 1)
    causal = ci <= ri
    lower = causal.astype(F32)
    upper = (ri <= ci).astype(F32)
    col_lane = lax.broadcasted_iota(jnp.int32, (1, LANES), 1)
    col_row = lax.broadcasted_iota(jnp.int32, (CHUNK, 1), 0)
    row_sub = lax.broadcasted_iota(jnp.int32, (SUBLANES, 1), 0)
    row_lane = lax.broadcasted_iota(jnp.int32, (1, CHUNK), 1)
    scale = M_DH ** -0.5

    def chunk(c, carry):
        r0 = pl.multiple_of(c * CHUNK, CHUNK)
        pos0 = t * tt + r0
        gc = _gate_act(gc_ref[pl.ds(r0, CHUNK), :] + bc_ref[...], col_lane < M_HEADS)
        gc = jnp.where(pos0 + col_row >= PAD, gc, jnp.where(col_lane < M_HEADS, -jnp.inf, 0.0))
        fcol = jnp.where(col_lane < M_HEADS, 0.0, gc)
        bcum_c = jnp.dot(lower, fcol, precision=HIGHEST, preferred_element_type=F32)
        gr = _gate_act(gr_ref[c] + br_ref[...], row_sub < M_HEADS)
        gr = jnp.where(pos0 + row_lane >= PAD, gr, jnp.where(row_sub < M_HEADS, -jnp.inf, 0.0))
        frow = jnp.where(row_sub < M_HEADS, 0.0, gr)
        bcum_r = jnp.dot(frow, upper, precision=HIGHEST, preferred_element_type=F32)

        heads = range(M_HEADS)
        hs = [slice(h * M_DH, (h + 1) * M_DH) for h in heads]
        q = [q_ref[pl.ds(r0, CHUNK), hs[h]] for h in heads]
        k = [k_ref[pl.ds(r0, CHUNK), hs[h]] * scale for h in heads]
        vb = [v_ref[pl.ds(r0, CHUNK), hs[h]].astype(BF16) for h in heads]
        qb = [x.astype(BF16) for x in q]
        bcol = [bcum_c[:, M_HEADS + h:M_HEADS + h + 1] for h in heads]
        licol = [gc[:, h:h + 1] for h in heads]
        brow = [bcum_r[M_HEADS + h:M_HEADS + h + 1, :] for h in heads]
        lirow = [gr[h:h + 1, :] for h in heads]
        c_st = [c_ref[h] for h in heads]
        n_st = [n_ref[h] for h in heads]
        m_st = [m_ref[h][:, 0:1] for h in heads]

        g = [brow[h][:, CHUNK - 1:CHUNK] for h in heads]
        dmat = [jnp.where(causal, bcol[h] - brow[h] + lirow[h], -jnp.inf) for h in heads]
        inter = [bcol[h] + m_st[h] for h in heads]
        m_row = [jnp.maximum(inter[h], jnp.max(dmat[h], axis=-1, keepdims=True)) for h in heads]
        qk = [_bdot_nt(qb[h], k[h]) for h in heads]
        qc = [_bdot(qb[h], c_st[h]) for h in heads]
        s = [qk[h] * jnp.exp(dmat[h] - m_row[h]) for h in heads]
        w_inter = [jnp.exp(inter[h] - m_row[h]) for h in heads]
        sv = [_bdot(s[h], vb[h]) for h in heads]
        a_row = [g[h] - brow[h] + lirow[h] for h in heads]
        m_new = [jnp.maximum(g[h] + m_st[h], jnp.max(a_row[h], axis=-1, keepdims=True))
                 for h in heads]
        decay = [jnp.exp(g[h] + m_st[h] - m_new[h]) for h in heads]
        kw = [k[h] * jnp.exp(g[h] - bcol[h] + licol[h] - m_new[h]) for h in heads]
        kv = [_bdot_tn(kw[h], vb[h]) for h in heads]
        for h in heads:
            c_ref[h] = decay[h] * c_st[h] + kv[h]
            n_ref[h] = decay[h] * n_st[h] + jnp.sum(kw[h], axis=0, keepdims=True)
            m_ref[h] = jnp.broadcast_to(m_new[h], (1, LANES))
        for h in heads:
            num = sv[h] + w_inter[h] * qc[h]
            den = (jnp.sum(s[h], axis=-1, keepdims=True)
                   + w_inter[h] * jnp.sum(q[h] * n_st[h], axis=-1, keepdims=True))
            hc = num / jnp.maximum(jnp.abs(den), jnp.exp(-m_row[h]))
            hn = hc * lax.rsqrt(jnp.mean(hc * hc, axis=-1, keepdims=True) + NORM_EPS)
            og = og_ref[pl.ds(r0, CHUNK), hs[h]]
            out_ref[pl.ds(r0, CHUNK), hs[h]] = (hn * mn_ref[:, hs[h]] * _sigmoid(og)).astype(BF16)
        return carry

    lax.fori_loop(0, tt // CHUNK, chunk, 0)


def _mlstm(z, gt_chunks, bias_col, bias_row, m_norm, batch, lp):
    m = z.shape[0]
    tt = SEQ_TILE
    nt = lp // tt
    row = lambda b, t: b * nt + t
    gate_col_block = 4 * M_W // LANES + (3 * R_W + 256) // LANES
    in_specs = [
        pl.BlockSpec((tt, M_W), lambda b, t: (row(b, t), 0)),
        pl.BlockSpec((tt, M_W), lambda b, t: (row(b, t), 1)),
        pl.BlockSpec((tt, M_W), lambda b, t: (row(b, t), 2)),
        pl.BlockSpec((tt, M_W), lambda b, t: (row(b, t), 3)),
        pl.BlockSpec((tt, LANES), lambda b, t: (row(b, t), gate_col_block)),
        pl.BlockSpec((tt // CHUNK, SUBLANES, CHUNK), lambda b, t: (row(b, t), 0, 0)),
        _const_spec((1, LANES)), _const_spec((SUBLANES, CHUNK)), _const_spec((1, M_W)),
    ]
    return pl.pallas_call(
        _mlstm_kernel, grid=(batch, nt), in_specs=in_specs,
        out_specs=pl.BlockSpec((tt, M_W), lambda b, t: (row(b, t), 0)),
        out_shape=jax.ShapeDtypeStruct((m, M_W), BF16),
        scratch_shapes=[pltpu.VMEM((M_HEADS, M_DH, M_DH), F32),
                        pltpu.VMEM((M_HEADS, 1, M_DH), F32),
                        pltpu.VMEM((M_HEADS, 1, LANES), F32)],
        compiler_params=pltpu.CompilerParams(dimension_semantics=("arbitrary", "arbitrary"),
                                             vmem_limit_bytes=VMEM_LIMIT),
        name="mlstm")(z, z, z, z, z, gt_chunks, bias_col, bias_row, m_norm)


def _rwkv_kernel(r_ref, k_ref, v_ref, wa_ref, g_ref, mur_ref, muk_ref, muv_ref, muwa_ref, mug_ref,
                 w0_ref, w2_ref, a0_ref, a2_ref, g2_ref, kk_ref, ka_ref, rk_ref, lnw_ref, lnb_ref,
                 out_ref, cr_ref, ck_ref, cv_ref, cwa_ref, cg_ref, st_ref):
    t = pl.program_id(1)
    tt = r_ref.shape[0]
    two_c = 2 * CHUNK

    @pl.when(t == 0)
    def _():
        for ref in (cr_ref, ck_ref, cv_ref, cwa_ref, cg_ref, st_ref):
            ref[...] = jnp.zeros_like(ref)

    row_i = lax.broadcasted_iota(jnp.int32, (CHUNK, 1), 0)
    row_8 = lax.broadcasted_iota(jnp.int32, (SUBLANES, 1), 0)
    lo = lax.broadcasted_iota(jnp.int32, (1, LANES), 1) < R_DH
    ri = lax.broadcasted_iota(jnp.int32, (CHUNK, CHUNK), 0)
    ci = lax.broadcasted_iota(jnp.int32, (CHUNK, CHUNK), 1)
    lower = (ci <= ri).astype(BF16)
    r4 = lax.broadcasted_iota(jnp.int32, (4 * CHUNK, 4 * CHUNK), 0)
    c4 = lax.broadcasted_iota(jnp.int32, (4 * CHUNK, 4 * CHUNK), 1)
    same_head = ((r4 // CHUNK) % 2) == ((c4 // CHUNK) % 2)
    below = (c4 % CHUNK < r4 % CHUNK) | ((r4 >= two_c) & (c4 % CHUNK == r4 % CHUNK))
    keep = jnp.where(same_head & below, 1.0, 0.0).astype(BF16)
    r2 = lax.broadcasted_iota(jnp.int32, (two_c, two_c), 0)
    c2 = lax.broadcasted_iota(jnp.int32, (two_c, two_c), 1)
    eye2 = (r2 == c2).astype(F32)
    decay_scale = float(np.exp(-0.5))

    def head_sum(x):
        parts = []
        for p in range(R_PAIRS):
            xp = x[:, p * LANES:(p + 1) * LANES]
            s0 = jnp.sum(jnp.where(lo, xp, 0.0), axis=-1, keepdims=True)
            s1 = jnp.sum(jnp.where(lo, 0.0, xp), axis=-1, keepdims=True)
            parts.append(jnp.where(lo, s0, s1))
        return jnp.concatenate(parts, axis=1)

    def stack_heads(x):
        return jnp.concatenate([jnp.where(lo, x, 0.0), jnp.where(lo, 0.0, x)], axis=0)

    def cumsum_rows(x):
        hi = x.astype(BF16)
        rest = x - hi.astype(F32)
        mid = rest.astype(BF16)
        low = (rest - mid.astype(F32)).astype(BF16)
        dot = lambda y: jnp.dot(lower, y, preferred_element_type=F32)
        return dot(hi) + dot(mid) + dot(low)

    def chunk(c, carry):
        r0 = pl.multiple_of(c * CHUNK, CHUNK)

        def shift_mix(x_ref, c_ref, mu_ref):
            x = x_ref[pl.ds(r0, CHUNK), :]
            prev = pltpu.roll(x, 1, axis=0)
            head = jnp.where(row_8 == 0, c_ref[SUBLANES - 1:SUBLANES, :], prev[:SUBLANES])
            prev = jnp.concatenate([head, prev[SUBLANES:]], axis=0)
            c_ref[...] = x[CHUNK - SUBLANES:CHUNK, :]
            return x + (prev - x) * mu_ref[...]

        rr = shift_mix(r_ref, cr_ref, mur_ref)
        kr = shift_mix(k_ref, ck_ref, muk_ref)
        vr = shift_mix(v_ref, cv_ref, muv_ref)
        xwa = shift_mix(wa_ref, cwa_ref, muwa_ref)
        xg = shift_mix(g_ref, cg_ref, mug_ref)

        lw = -decay_scale * _sigmoid(w0_ref[...] + _bdot(jnp.tanh(xwa), w2_ref[...]))
        aa = _sigmoid(a0_ref[...] + _bdot(xwa, a2_ref[...]))
        gg = _bdot(_sigmoid(xg), g2_ref[...])
        kk = kr * kk_ref[...]
        kk = kk * lax.rsqrt(jnp.maximum(head_sum(kk * kk), 1e-24))
        k2 = kr * (1.0 + (aa - 1.0) * ka_ref[...])

        cum = cumsum_rows(lw)
        e_pos = jnp.exp(cum)
        e_neg = jnp.exp(-cum)
        alpha = -kk * jnp.exp(cum - lw)
        beta = kk * aa * e_neg
        kt = k2 * e_neg
        rb = rr * e_pos
        p_end = e_pos[CHUNK - 1:CHUNK, :]
        beta_end = beta * p_end
        kt_end = kt * p_end

        pairs = range(R_PAIRS)
        pair = lambda x, p: x[:, p * LANES:(p + 1) * LANES]
        bf = lambda x: x.astype(BF16)
        left = [jnp.concatenate([stack_heads(pair(alpha, p)), stack_heads(pair(rb, p))],
                                axis=0).astype(BF16) for p in pairs]
        right = [jnp.concatenate([pair(beta, p)] * 2 + [pair(kt, p)] * 2, axis=0).astype(BF16)
                 for p in pairs]
        v_st = [stack_heads(pair(vr, p)).astype(BF16) for p in pairs]
        end_st = [jnp.concatenate([stack_heads(pair(beta_end, p)), stack_heads(pair(kt_end, p))],
                                  axis=0).astype(BF16) for p in pairs]
        prod = [_bdot_nt(left[p], right[p]).astype(BF16) * keep for p in pairs]
        a_ab = [x[:two_c, :two_c] for x in prod]
        w_rb = [x[two_c:, :two_c] for x in prod]
        to_v = [x[:, two_c:] for x in prod]

        tinv = [eye2 + a.astype(F32) for a in a_ab]
        apow = [_bdot(a, a) for a in a_ab]
        for _ in range(4):
            both = [_bdot(apow[p], jnp.concatenate([bf(apow[p]), bf(tinv[p])], axis=1))
                    for p in pairs]
            apow = [x[:, :two_c] for x in both]
            tinv = [tinv[p] + both[p][:, two_c:] for p in pairs]
        tinv = [tinv[p] + _bdot(apow[p], tinv[p]) for p in pairs]

        st = [st_ref[p] for p in pairs]
        on_s = [_bdot_nt(left[p], st[p]) for p in pairs]
        on_v = [_bdot(to_v[p], v_st[p]) for p in pairs]
        u = [_bdot(tinv[p], on_s[p][:two_c] + on_v[p][:two_c]) for p in pairs]
        o_st = [on_s[p][two_c:] + on_v[p][two_c:] + _bdot(w_rb[p], u[p]) for p in pairs]
        for p in pairs:
            uv = jnp.concatenate([bf(u[p]), v_st[p]], axis=0)
            st_ref[p] = pair(p_end, p) * st[p] + _bdot_tn(uv, end_st[p])
        o = jnp.concatenate([x[:CHUNK] + x[CHUNK:] for x in o_st], axis=1)

        mean = head_sum(o) * (1.0 / R_DH)
        oc = o - mean
        var = head_sum(oc * oc) * (1.0 / R_DH)
        y = oc * lax.rsqrt(var + R_LN_EPS) * lnw_ref[...] + lnb_ref[...]
        y = y + head_sum(rr * k2 * rk_ref[...]) * vr
        valid = (t * tt + r0 + row_i) >= PAD
        out_ref[pl.ds(r0, CHUNK), :] = jnp.where(valid, y * gg, 0.0).astype(BF16)
        return carry

    lax.fori_loop(0, tt // CHUNK, chunk, 0)


def _rwkv(z, vecs, mats, batch, lp):
    m = z.shape[0]
    tt = SEQ_TILE
    nt = lp // tt
    row = lambda b, t: b * nt + t
    base = 4 * M_W
    in_specs = [
        pl.BlockSpec((tt, R_W), lambda b, t: (row(b, t), base // R_W)),
        pl.BlockSpec((tt, R_W), lambda b, t: (row(b, t), base // R_W + 1)),
        pl.BlockSpec((tt, R_W), lambda b, t: (row(b, t), base // R_W + 2)),
        pl.BlockSpec((tt, LANES), lambda b, t: (row(b, t), (base + 3 * R_W) // LANES)),
        pl.BlockSpec((tt, LANES), lambda b, t: (row(b, t), (base + 3 * R_W) // LANES + 1)),
    ]
    mu_r, mu_k, mu_v, mu_wa, mu_g, w0, a0, k_k, k_a, r_k, ln_w, ln_b = vecs
    w2, a2, g2 = mats
    args = [mu_r, mu_k, mu_v, mu_wa, mu_g, w0, w2, a0, a2, g2, k_k, k_a, r_k, ln_w, ln_b]
    in_specs += [_const_spec(a.shape) for a in args]
    return pl.pallas_call(
        _rwkv_kernel, grid=(batch, nt), in_specs=in_specs,
        out_specs=pl.BlockSpec((tt, R_W), lambda b, t: (row(b, t), 0)),
        out_shape=jax.ShapeDtypeStruct((m, R_W), BF16),
        scratch_shapes=[pltpu.VMEM((SUBLANES, R_W), F32), pltpu.VMEM((SUBLANES, R_W), F32),
                        pltpu.VMEM((SUBLANES, R_W), F32), pltpu.VMEM((SUBLANES, LANES), F32),
                        pltpu.VMEM((SUBLANES, LANES), F32),
                        pltpu.VMEM((R_PAIRS, LANES, LANES), F32)],
        compiler_params=pltpu.CompilerParams(dimension_semantics=("arbitrary", "arbitrary"),
                                             vmem_limit_bytes=VMEM_LIMIT),
        name="rwkv7")(z, z, z, z, z, *args)


def _retention_kernel(q_ref, k_ref, v_ref, gate_ref, cos_ref, sin_ref, out_ref, st_ref):
    t = pl.program_id(1)
    c = q_ref.shape[0]

    @pl.when(t == 0)
    def _():
        st_ref[...] = jnp.zeros_like(st_ref)

    ri = lax.broadcasted_iota(jnp.int32, (c, c), 0)
    ci = lax.broadcasted_iota(jnp.int32, (c, c), 1)
    diff = (ri - ci).astype(F32)
    idx = lax.broadcasted_iota(jnp.int32, (c, 1), 0).astype(F32)
    cos = cos_ref[...]
    sin = sin_ref[...]
    half = T_DK // 2

    def rotary(x):
        swapped = jnp.concatenate([x[:, half:], x[:, :half]], axis=1)
        return x * cos + swapped * sin

    for h in range(T_HEADS):
        log_gamma = float(np.log(1.0 - 2.0 ** (-5.0 - h)))
        ks = slice(h * T_DK, (h + 1) * T_DK)
        vs = slice(h * T_DV, (h + 1) * T_DV)
        q = rotary(q_ref[:, ks])
        k = rotary(k_ref[:, ks]) * (T_DK ** -0.5)
        vb = v_ref[:, vs].astype(BF16)
        intra = jnp.where(diff >= 0, jnp.exp(log_gamma * jnp.maximum(diff, 0.0)), 0.0)
        q_dec = jnp.exp(log_gamma * (idx + 1.0))
        k_dec = jnp.exp(log_gamma * (c - 1.0 - idx))
        st = st_ref[h]
        s = _bdot_nt(q, k) * intra
        o = _bdot(s, vb) + _bdot(q * q_dec, st)
        st_ref[h] = float(np.exp(log_gamma * c)) * st + _bdot_tn(k * k_dec, vb)

        mu = jnp.mean(o, axis=-1, keepdims=True)
        oc = o - mu
        var = jnp.mean(oc * oc, axis=-1, keepdims=True)
        gate = gate_ref[:, vs]
        out_ref[:, vs] = (oc * lax.rsqrt(var + NORM_EPS) * (gate * _sigmoid(gate))).astype(BF16)


def _retention(z, cos, sin, batch, lp):
    m = z.shape[0]
    tt = SEQ_TILE
    nt = lp // tt
    row = lambda b, t: b * nt + t
    in_specs = [
        pl.BlockSpec((tt, D_MODEL), lambda b, t: (row(b, t), 0)),
        pl.BlockSpec((tt, D_MODEL), lambda b, t: (row(b, t), 1)),
        pl.BlockSpec((tt, T_WV), lambda b, t: (row(b, t), 1)),
        pl.BlockSpec((tt, T_WV), lambda b, t: (row(b, t), 2)),
        pl.BlockSpec((tt, T_DK), lambda b, t: (t, 0)),
        pl.BlockSpec((tt, T_DK), lambda b, t: (t, 0)),
    ]
    return pl.pallas_call(
        _retention_kernel, grid=(batch, nt), in_specs=in_specs,
        out_specs=pl.BlockSpec((tt, T_WV), lambda b, t: (row(b, t), 0)),
        out_shape=jax.ShapeDtypeStruct((m, T_WV), BF16),
        scratch_shapes=[pltpu.VMEM((T_HEADS, T_DK, T_DV), F32)],
        compiler_params=pltpu.CompilerParams(dimension_semantics=("arbitrary", "arbitrary"),
                                             vmem_limit_bytes=VMEM_LIMIT),
        name="retention")(z, z, z, z, cos, sin)


def _out_proj_kernel(n_in, *refs):
    h_ref = refs[0]
    a_refs = refs[1:1 + n_in]
    w_refs = refs[1 + n_in:1 + 2 * n_in]
    o_ref = refs[1 + 2 * n_in]
    acc = h_ref[...]
    for a_ref, w_ref in zip(a_refs, w_refs):
        acc = acc + jnp.dot(a_ref[...], w_ref[...], preferred_element_type=F32)
    o_ref[...] = acc


def _out_proj(h, acts, weights):
    m, d = h.shape
    tm = ROW_TILE
    in_specs = [pl.BlockSpec((tm, d), lambda i: (i, 0))]
    in_specs += [pl.BlockSpec((tm, a.shape[1]), lambda i: (i, 0)) for a in acts]
    in_specs += [_const_spec(w.shape) for w in weights]
    return pl.pallas_call(
        functools.partial(_out_proj_kernel, len(acts)), grid=(m // tm,), in_specs=in_specs,
        out_specs=pl.BlockSpec((tm, d), lambda i: (i, 0)),
        out_shape=jax.ShapeDtypeStruct((m, d), F32),
        compiler_params=pltpu.CompilerParams(dimension_semantics=("arbitrary",),
                                             vmem_limit_bytes=VMEM_LIMIT),
        name="out_proj")(h, *acts, *weights)


def _ffn_kernel(h_ref, g_ref, wv_ref, wg_ref, cw_ref, cb_ref, wd_ref, o_ref,
                carry_ref, hn_ref, act_ref):
    tm = h_ref.shape[0]

    @pl.when(pl.program_id(0) == 0)
    def _():
        carry_ref[...] = jnp.zeros_like(carry_ref)

    hn_ref[...] = _rmsnorm_bf16(h_ref[...], g_ref[...])
    row = lax.broadcasted_iota(jnp.int32, (SUBLANES, 1), 0)

    for c in range(N_FF_CHUNKS):
        cs = slice(c * FF_CHUNK, (c + 1) * FF_CHUNK)
        hn = hn_ref[...]
        val = jnp.dot(hn, wv_ref[:, cs], preferred_element_type=F32)
        gate = jnp.dot(hn, wg_ref[:, cs], preferred_element_type=F32)
        tail = carry_ref[:, cs]
        prev1 = pltpu.roll(gate, 1, axis=0)
        prev2 = pltpu.roll(gate, 2, axis=0)
        head1 = jnp.where(row == 0, tail[SUBLANES - 1:SUBLANES, :], prev1[:SUBLANES])
        head2 = jnp.where(row == 0, tail[SUBLANES - 2:SUBLANES - 1, :], prev2[:SUBLANES])
        head2 = jnp.where(row == 1, tail[SUBLANES - 1:SUBLANES, :], head2)
        prev1 = jnp.concatenate([head1, prev1[SUBLANES:]], axis=0)
        prev2 = jnp.concatenate([head2, prev2[SUBLANES:]], axis=0)
        carry_ref[:, cs] = gate[tm - SUBLANES:tm, :]
        conv = (cw_ref[2:3, cs] * gate + cw_ref[1:2, cs] * prev1 + cw_ref[0:1, cs] * prev2
                + cb_ref[:, cs])
        act_ref[:, cs] = (conv * _sigmoid(conv) * val).astype(BF16)
    o_ref[...] = h_ref[...] + jnp.dot(act_ref[...], wd_ref[...], preferred_element_type=F32)


def _ffn(h, g, wv, wg, cw, cb, wd):
    m, d = h.shape
    tm = ROW_TILE
    in_specs = [pl.BlockSpec((tm, d), lambda i: (i, 0)), _const_spec((1, d))]
    in_specs += [_const_spec(a.shape) for a in (wv, wg, cw, cb, wd)]
    return pl.pallas_call(
        _ffn_kernel, grid=(m // tm,), in_specs=in_specs,
        out_specs=pl.BlockSpec((tm, d), lambda i: (i, 0)),
        out_shape=jax.ShapeDtypeStruct((m, d), F32),
        scratch_shapes=[pltpu.VMEM((SUBLANES, D_FF), F32),
                        pltpu.VMEM((tm, d), BF16),
                        pltpu.VMEM((tm, D_FF), BF16)],
        compiler_params=pltpu.CompilerParams(dimension_semantics=("arbitrary",),
                                             vmem_limit_bytes=VMEM_LIMIT),
        name="conv_ffn")(h, g, wv, wg, cw, cb, wd)


def _final_norm_kernel(h_ref, g_ref, o_ref):
    x = h_ref[...]
    ms = jnp.mean(x * x, axis=-1, keepdims=True)
    o_ref[0] = x * lax.rsqrt(ms + NORM_EPS) * g_ref[...]


def _final_norm(h, g, batch, lp):
    d = h.shape[1]
    per_batch = lp // FRONT
    seq = lp - FRONT
    return pl.pallas_call(
        _final_norm_kernel, grid=(batch, per_batch - 1),
        in_specs=[pl.BlockSpec((FRONT, d), lambda b, j: (b * per_batch + j + 1, 0)),
                  _const_spec((1, d))],
        out_specs=pl.BlockSpec((1, FRONT, d), lambda b, j: (b, j, 0)),
        out_shape=jax.ShapeDtypeStruct((batch, seq, d), F32),
        compiler_params=pltpu.CompilerParams(dimension_semantics=("arbitrary", "arbitrary")),
        name="final_norm")(h, g)


def _ffn_weights(w_up, conv_w, conv_b, w_down):
    wv = w_up[:, :D_FF].astype(BF16)
    wg = w_up[:, D_FF:].astype(BF16)
    cw = jnp.pad(conv_w, ((0, SUBLANES - conv_w.shape[0]), (0, 0)))
    return wv, wg, cw, conv_b.reshape(1, D_FF), w_down.astype(BF16)


def kernel(x, meta_tokens, norm_mix, norm_ffn, norm_final, e_w_in, e_w_out, m_b_i, m_b_f, m_norm,
           r_mu, r_w0, r_w2, r_a0, r_a2, r_g2, r_k_k, r_k_a, r_r_k, r_ln_w, r_ln_b, o_w_in, o_w_out,
           f_w_up, f_conv_w, f_conv_b, f_w_down):
    batch, seq, d = x.shape
    lp = seq + FRONT
    assert d == D_MODEL and lp % SEQ_TILE == 0 and (batch * lp) % ROW_TILE == 0
    m = batch * lp
    row = lambda v: v.reshape(1, -1).astype(F32)

    meta = jnp.broadcast_to(meta_tokens[None].astype(x.dtype), (batch, N_META, d))
    h = jnp.concatenate([jnp.zeros((batch, PAD, d), x.dtype), meta, x], axis=1).reshape(m, d)

    w_in = e_w_in[0]
    n_m = 4 * M_W
    gates_w = w_in[:, n_m:n_m + 2 * M_HEADS]
    w0 = jnp.concatenate([w_in[:, :n_m], w_in[:, n_m + 2 * M_HEADS:], gates_w,
                          jnp.zeros((d, LANES - 2 * M_HEADS), F32)], axis=1).astype(BF16)
    z, gt = _norm_proj(h, row(norm_mix[0]), w0, gates_w.T.astype(BF16))
    gt_chunks = gt.reshape(SUBLANES, m // CHUNK, CHUNK).transpose(1, 0, 2)
    gate_bias = jnp.concatenate([m_b_i[0], m_b_f[0]])
    bias_col = jnp.pad(gate_bias, (0, LANES - 2 * M_HEADS)).reshape(1, LANES)
    bias_row = jnp.broadcast_to(gate_bias[:, None], (SUBLANES, CHUNK))
    mix_m = _mlstm(z, gt_chunks, bias_col, bias_row, row(m_norm[0]), batch, lp)

    mu = r_mu[0]
    rank_wa = r_w2.shape[1] + r_a2.shape[1]
    vecs = [row(mu[:R_W]), row(mu[R_W:2 * R_W]), row(mu[2 * R_W:3 * R_W]),
            row(mu[3 * R_W:3 * R_W + rank_wa]), row(mu[3 * R_W + rank_wa:]),
            row(r_w0[0]), row(r_a0[0]), row(r_k_k[0]), row(r_k_a[0]), row(r_r_k[0]),
            row(r_ln_w[0]), row(r_ln_b[0])]
    w2 = jnp.concatenate([r_w2[0], jnp.zeros_like(r_a2[0])], axis=0).astype(BF16)
    a2 = jnp.concatenate([jnp.zeros_like(r_w2[0]), r_a2[0]], axis=0).astype(BF16)
    mix_r = _rwkv(z, vecs, [w2, a2, r_g2[0].astype(BF16)], batch, lp)

    w_out = e_w_out[0].astype(BF16)
    h = _out_proj(h, [mix_m, mix_r], [w_out[:M_W], w_out[M_W:]])
    h = _ffn(h, row(norm_ffn[0]), *_ffn_weights(f_w_up[0], f_conv_w[0], f_conv_b[0], f_w_down[0]))

    w_in = o_w_in[0]
    perm = np.concatenate([np.arange(0, T_DK, 2), np.arange(1, T_DK, 2)])
    qk_cols = np.concatenate([hh * T_DK + perm for hh in range(T_HEADS)])
    w1 = jnp.concatenate([w_in[:, qk_cols], w_in[:, D_MODEL + qk_cols], w_in[:, 2 * D_MODEL:]],
                         axis=1).astype(BF16)
    z = _norm_proj(h, row(norm_mix[1]), w1)
    inv = 1.0 / (ROPE_BASE ** jnp.linspace(0.0, 1.0, T_DK // 2, dtype=F32))
    pos = jnp.arange(lp, dtype=F32) - PAD
    ang = pos[:, None] * inv[None, :]
    cos = jnp.concatenate([jnp.cos(ang), jnp.cos(ang)], axis=1)
    sin = jnp.concatenate([-jnp.sin(ang), jnp.sin(ang)], axis=1)
    o = _retention(z, cos, sin, batch, lp)
    h = _out_proj(h, [o], [o_w_out[0].astype(BF16)])
    h = _ffn(h, row(norm_ffn[1]), *_ffn_weights(f_w_up[1], f_conv_w[1], f_conv_b[1], f_w_down[1]))

    return _final_norm(h, row(norm_final), batch, lp)
```

```python
import functools

import numpy as np
import jax
import jax.numpy as jnp
from jax import lax
from jax.experimental import pallas as pl
from jax.experimental.pallas import tpu as pltpu

F32 = jnp.float32
BF16 = jnp.bfloat16
HIGHEST = lax.Precision.HIGHEST

D_MODEL = 1024
N_META = 16
NORM_EPS = 1e-6
M_HEADS = 4
M_DH = 128
M_W = M_HEADS * M_DH
GATE_CAP = 15.0
R_DH = 64
R_HEADS = 8
R_W = R_HEADS * R_DH
R_PAIRS = R_HEADS // 2
R_LN_EPS = 64e-5
T_HEADS = 4
T_DK = 256
T_DV = 512
T_WV = T_HEADS * T_DV
ROPE_BASE = 10000.0
D_FF = 2816
FF_CHUNK = 256
N_FF_CHUNKS = D_FF // FF_CHUNK

LANES = 128
SUBLANES = 8
FRONT = 128
PAD = FRONT - N_META
CHUNK = 64
ROW_TILE = 512
SEQ_TILE = 384
LONG_TILE = 1408
DOUBLINGS_AHEAD = 1
VMEM_LIMIT = 56 * 1024 * 1024


def _const_spec(shape):
    nd = len(shape)
    return pl.BlockSpec(shape, lambda *_: (0,) * nd, pipeline_mode=pl.Buffered(1))


def _sigmoid(x):
    return 1.0 / (1.0 + jnp.exp(-x))


def _softplus(x):
    return jnp.maximum(x, 0.0) + jnp.log1p(jnp.exp(-jnp.abs(x)))


def _bdot(a, b):
    return jnp.dot(a.astype(BF16), b.astype(BF16), preferred_element_type=F32)


def _bdot_nt(a, b):
    return lax.dot_general(a.astype(BF16), b.astype(BF16), (((1,), (1,)), ((), ())),
                           preferred_element_type=F32)


def _bdot_tn(a, b):
    return lax.dot_general(a.astype(BF16), b.astype(BF16), (((0,), (0,)), ((), ())),
                           preferred_element_type=F32)


def _split3(x):
    hi = x.astype(BF16)
    rest = x - hi.astype(F32)
    mid = rest.astype(BF16)
    return hi, mid, (rest - mid.astype(F32)).astype(BF16)


def _select_dot(sel, x):
    hi, mid, low = _split3(x)
    dot = lambda y: jnp.dot(sel, y, preferred_element_type=F32)
    return dot(hi) + dot(mid) + dot(low)


def _dot_select(x, sel):
    hi, mid, low = _split3(x)
    dot = lambda y: jnp.dot(y, sel, preferred_element_type=F32)
    return dot(hi) + dot(mid) + dot(low)


def _run_stages(*stages):
    live = list(stages)
    while live:
        live = [s for s in live if next(s, live) is not live]


def _rmsnorm_bf16(x, g):
    ms = jnp.mean(x * x, axis=-1, keepdims=True)
    return (x * lax.rsqrt(ms + NORM_EPS) * g).astype(BF16)


def _col_chunks(n, width=512):
    return [(c, min(width, n - c)) for c in range(0, n, width)]


def _norm_proj_kernel(h_ref, g_ref, w_ref, o_ref, hn_ref):
    hn_ref[...] = _rmsnorm_bf16(h_ref[...], g_ref[...])
    for c0, cw in _col_chunks(w_ref.shape[1]):
        o_ref[:, c0:c0 + cw] = jnp.dot(hn_ref[...], w_ref[:, c0:c0 + cw],
                                       preferred_element_type=F32)


def _norm_proj_gates_kernel(h_ref, g_ref, w_ref, wgt_ref, o_ref, gt_ref, hn_ref):
    _norm_proj_kernel(h_ref, g_ref, w_ref, o_ref, hn_ref)
    gt_ref[...] = lax.dot_general(wgt_ref[...], hn_ref[...], (((1,), (1,)), ((), ())),
                                  preferred_element_type=F32)


def _norm_proj(h, g, w, wgt=None):
    m, d = h.shape
    n = w.shape[1]
    tm = ROW_TILE if n <= 4096 else ROW_TILE // 2
    in_specs = [pl.BlockSpec((tm, d), lambda i: (i, 0)), _const_spec((1, d)), _const_spec((d, n))]
    params = pltpu.CompilerParams(dimension_semantics=("arbitrary",), vmem_limit_bytes=VMEM_LIMIT)
    if wgt is None:
        return pl.pallas_call(
            _norm_proj_kernel, grid=(m // tm,), in_specs=in_specs,
            out_specs=pl.BlockSpec((tm, n), lambda i: (i, 0)),
            out_shape=jax.ShapeDtypeStruct((m, n), F32),
            scratch_shapes=[pltpu.VMEM((tm, d), BF16)],
            compiler_params=params, name="norm_proj")(h, g, w)
    return pl.pallas_call(
        _norm_proj_gates_kernel, grid=(m // tm,),
        in_specs=in_specs + [_const_spec(wgt.shape)],
        out_specs=[pl.BlockSpec((tm, n), lambda i: (i, 0)),
                   pl.BlockSpec((SUBLANES, tm), lambda i: (0, i))],
        out_shape=[jax.ShapeDtypeStruct((m, n), F32), jax.ShapeDtypeStruct((SUBLANES, m), F32)],
        scratch_shapes=[pltpu.VMEM((tm, d), BF16)],
        compiler_params=params, name="norm_proj_gates")(h, g, w, wgt)


def _gate_act(z, is_input_gate):
    capped = GATE_CAP * jnp.tanh(z / GATE_CAP)
    return jnp.where(is_input_gate, capped, -_softplus(-capped))


def _mlstm_kernel(q_ref, k_ref, v_ref, og_ref, gc_ref, gr_ref, bc_ref, br_ref, mn_ref,
                  out_ref, c_ref, m_ref, h_sv, h_kv, h_q, h_maxd, h_bcum, h_row, h_gate):
    t = pl.program_id(1)
    tt = q_ref.shape[0]

    @pl.when(t == 0)
    def _():
        c_ref[...] = jnp.zeros_like(c_ref)
        m_ref[...] = jnp.zeros_like(m_ref)

    ri = lax.broadcasted_iota(jnp.int32, (CHUNK, CHUNK), 0)
    ci = lax.broadcasted_iota(jnp.int32, (CHUNK, CHUNK), 1)
    causal = ci <= ri
    lower = jnp.where(causal, 1.0, 0.0).astype(BF16)
    upper = jnp.where(ri <= ci, 1.0, 0.0).astype(BF16)
    n_blocks = 2 * M_HEADS
    sel_r = lax.broadcasted_iota(jnp.int32, (LANES, n_blocks * LANES), 0)
    sel_c = lax.broadcasted_iota(jnp.int32, (LANES, n_blocks * LANES), 1)
    spread = jnp.where(sel_r == sel_c // LANES, 1.0, 0.0).astype(BF16)
    ones_blk = jnp.ones((CHUNK, LANES), BF16)
    col_lane = lax.broadcasted_iota(jnp.int32, (1, LANES), 1)
    col_row = lax.broadcasted_iota(jnp.int32, (CHUNK, 1), 0)
    row_sub = lax.broadcasted_iota(jnp.int32, (SUBLANES, 1), 0)
    row_lane = lax.broadcasted_iota(jnp.int32, (1, CHUNK), 1)
    scale = M_DH ** -0.5
    heads = range(M_HEADS)
    hs = [slice(h * M_DH, (h + 1) * M_DH) for h in heads]
    blk = lambda x, j: x[:, j * LANES:(j + 1) * LANES]
    row0 = lambda c: c * CHUNK if isinstance(c, int) else pl.multiple_of(c * CHUNK, CHUNK)
    finite = lambda x: jnp.where(x == -jnp.inf, 0.0, x)

    def stage_a(c):
        r0 = row0(c)
        pos0 = t * tt + r0
        gact = _gate_act(gc_ref[pl.ds(r0, CHUNK), :] + bc_ref[...], col_lane < M_HEADS)
        spread_g = _dot_select(gact, spread)
        valid = pos0 + col_row >= PAD
        li_b = [jnp.where(valid, blk(spread_g, h), -jnp.inf) for h in heads]
        lf_b = jnp.where(valid, spread_g[:, M_HEADS * LANES:], 0.0)
        bcum_all = _select_dot(lower, lf_b)
        bcum_b = [blk(bcum_all, h) for h in heads]
        gr = _gate_act(gr_ref[c] + br_ref[...], row_sub < M_HEADS)
        gr = jnp.where(pos0 + row_lane >= PAD, gr, jnp.where(row_sub < M_HEADS, -jnp.inf, 0.0))
        bcum_r = _dot_select(jnp.where(row_sub < M_HEADS, 0.0, gr), upper)
        yield
        qb = [q_ref[pl.ds(r0, CHUNK), hs[h]].astype(BF16) for h in heads]
        k = [k_ref[pl.ds(r0, CHUNK), hs[h]] * scale for h in heads]
        vb = [jnp.concatenate([v_ref[pl.ds(r0, CHUNK), hs[h]].astype(BF16), ones_blk], axis=1)
              for h in heads]
        qk = [_bdot_nt(qb[h], k[h]) for h in heads]
        yield
        brow = [bcum_r[M_HEADS + h:M_HEADS + h + 1, :] for h in heads]
        lirow = [gr[h:h + 1, :] for h in heads]
        g_b = [bcum_b[h][CHUNK - 1:CHUNK, :] for h in heads]
        dmat = [jnp.where(causal, bcum_b[h][:, :CHUNK] - brow[h] + lirow[h], -jnp.inf)
                for h in heads]
        maxd = [jnp.max(dmat[h], axis=-1, keepdims=True) for h in heads]
        a_row = [g_b[h][:, :1] - brow[h] + lirow[h] for h in heads]
        amax = [jnp.max(a_row[h], axis=-1, keepdims=True) for h in heads]
        yield
        sp = [qk[h] * jnp.exp(dmat[h] - finite(maxd[h])) for h in heads]
        kwp = [k[h] * jnp.exp(g_b[h] - bcum_b[h] + li_b[h] - finite(amax[h])) for h in heads]
        svp = [_bdot(sp[h], vb[h]) for h in heads]
        yield
        kvp = [_bdot_tn(kwp[h], vb[h]) for h in heads]
        yield
        for h in heads:
            og = og_ref[pl.ds(r0, CHUNK), hs[h]]
            h_sv[h] = svp[h]
            h_kv[h] = kvp[h]
            h_q[h] = qb[h]
            h_maxd[h] = jnp.broadcast_to(maxd[h], (CHUNK, LANES))
            h_bcum[h] = bcum_b[h]
            h_row[h] = jnp.concatenate([jnp.broadcast_to(amax[h], (1, LANES)), g_b[h]], axis=0)
            h_gate[h] = mn_ref[:, hs[h]] * _sigmoid(og)

    def stage_b(c):
        r0 = row0(c)
        svp = [h_sv[h] for h in heads]
        kvp = [h_kv[h] for h in heads]
        qb = [h_q[h] for h in heads]
        maxd = [h_maxd[h] for h in heads]
        bcum_b = [h_bcum[h] for h in heads]
        amax = [h_row[h][0:1, :] for h in heads]
        g_b = [h_row[h][1:2, :] for h in heads]
        gate = [h_gate[h] for h in heads]
        c_st = [c_ref[h] for h in heads]
        m_st = [m_ref[h] for h in heads]
        qcn = [_bdot(qb[h], c_st[h]) for h in heads]
        inter = [bcum_b[h] + m_st[h] for h in heads]
        m_row = [jnp.maximum(inter[h], maxd[h]) for h in heads]
        yield
        m_new = [jnp.maximum(g_b[h] + m_st[h], amax[h]) for h in heads]
        for h in heads:
            decay = jnp.exp(g_b[h] + m_st[h] - m_new[h])[:, :1]
            grow = jnp.exp(finite(amax[h]) - m_new[h])[:, :1]
            c_ref[h] = decay * c_st[h] + grow * kvp[h]
            m_ref[h] = m_new[h]
        yield
        f_intra = [jnp.exp(finite(maxd[h]) - m_row[h]) for h in heads]
        w_inter = [jnp.exp(inter[h] - m_row[h]) for h in heads]
        yield
        for h in heads:
            num = f_intra[h] * svp[h][:, :M_DH] + w_inter[h] * qcn[h][:, :M_DH]
            den = f_intra[h] * svp[h][:, M_DH:] + w_inter[h] * qcn[h][:, M_DH:]
            hc = num / jnp.maximum(jnp.abs(den), jnp.exp(-m_row[h]))
            hn = hc * lax.rsqrt(jnp.mean(hc * hc, axis=-1, keepdims=True) + NORM_EPS)
            out_ref[pl.ds(r0, CHUNK), hs[h]] = (hn * gate[h]).astype(BF16)

    def steady(c, carry):
        _run_stages(stage_b(c), stage_a(c + 1))
        return carry

    n_chunks = tt // CHUNK
    _run_stages(stage_a(0))
    lax.fori_loop(0, n_chunks - 1, steady, 0)
    _run_stages(stage_b(n_chunks - 1))


def _mlstm(z, gt_chunks, bias_col, bias_row, m_norm, batch, lp):
    m = z.shape[0]
    tt = LONG_TILE if lp % LONG_TILE == 0 else SEQ_TILE
    nt = lp // tt
    row = lambda b, t: b * nt + t
    gate_col_block = 4 * M_W // LANES + (3 * R_W + 256) // LANES
    handoff = [pltpu.VMEM((M_HEADS, CHUNK, 2 * M_DH), F32),
               pltpu.VMEM((M_HEADS, M_DH, 2 * M_DH), F32),
               pltpu.VMEM((M_HEADS, CHUNK, M_DH), BF16),
               pltpu.VMEM((M_HEADS, CHUNK, LANES), F32),
               pltpu.VMEM((M_HEADS, CHUNK, LANES), F32),
               pltpu.VMEM((M_HEADS, 2, LANES), F32),
               pltpu.VMEM((M_HEADS, CHUNK, M_DH), F32)]
    in_specs = [
        pl.BlockSpec((tt, M_W), lambda b, t: (row(b, t), 0)),
        pl.BlockSpec((tt, M_W), lambda b, t: (row(b, t), 1)),
        pl.BlockSpec((tt, M_W), lambda b, t: (row(b, t), 2)),
        pl.BlockSpec((tt, M_W), lambda b, t: (row(b, t), 3)),
        pl.BlockSpec((tt, LANES), lambda b, t: (row(b, t), gate_col_block)),
        pl.BlockSpec((tt // CHUNK, SUBLANES, CHUNK), lambda b, t: (row(b, t), 0, 0)),
        _const_spec((1, LANES)), _const_spec((SUBLANES, CHUNK)), _const_spec((1, M_W)),
    ]
    return pl.pallas_call(
        _mlstm_kernel, grid=(batch, nt), in_specs=in_specs,
        out_specs=pl.BlockSpec((tt, M_W), lambda b, t: (row(b, t), 0)),
        out_shape=jax.ShapeDtypeStruct((m, M_W), BF16),
        scratch_shapes=[pltpu.VMEM((M_HEADS, M_DH, 2 * M_DH), F32),
                        pltpu.VMEM((M_HEADS, 1, LANES), F32)] + handoff,
        compiler_params=pltpu.CompilerParams(dimension_semantics=("arbitrary", "arbitrary"),
                                             vmem_limit_bytes=VMEM_LIMIT),
        name="mlstm")(z, z, z, z, z, gt_chunks, bias_col, bias_row, m_norm)


def _rwkv_kernel(r_ref, k_ref, v_ref, wa_ref, g_ref, mur_ref, muk_ref, muv_ref, muwa_ref, mug_ref,
                 w0_ref, w2_ref, a0_ref, a2_ref, g2_ref, kk_ref, ka_ref, rk_ref, lnw_ref, lnb_ref,
                 out_ref, cr_ref, ck_ref, cv_ref, cwa_ref, cg_ref, st_ref,
                 h_left, h_vst, h_end, h_wrb, h_onv, h_apow, h_tinv, h_pend, h_bonus, h_gg):
    t = pl.program_id(1)
    tt = r_ref.shape[0]
    two_c = 2 * CHUNK

    @pl.when(t == 0)
    def _():
        for ref in (cr_ref, ck_ref, cv_ref, cwa_ref, cg_ref, st_ref):
            ref[...] = jnp.zeros_like(ref)

    row_i = lax.broadcasted_iota(jnp.int32, (CHUNK, 1), 0)
    row_8 = lax.broadcasted_iota(jnp.int32, (SUBLANES, 1), 0)
    lo = lax.broadcasted_iota(jnp.int32, (1, LANES), 1) < R_DH
    ri = lax.broadcasted_iota(jnp.int32, (CHUNK, CHUNK), 0)
    ci = lax.broadcasted_iota(jnp.int32, (CHUNK, CHUNK), 1)
    lower = (ci <= ri).astype(BF16)
    r4 = lax.broadcasted_iota(jnp.int32, (4 * CHUNK, 4 * CHUNK), 0)
    c4 = lax.broadcasted_iota(jnp.int32, (4 * CHUNK, 4 * CHUNK), 1)
    same_head = ((r4 // CHUNK) % 2) == ((c4 // CHUNK) % 2)
    below = (c4 % CHUNK < r4 % CHUNK) | ((r4 >= two_c) & (c4 % CHUNK == r4 % CHUNK))
    keep = jnp.where(same_head & below, 1.0, 0.0).astype(BF16)
    r2 = lax.broadcasted_iota(jnp.int32, (two_c, two_c), 0)
    c2 = lax.broadcasted_iota(jnp.int32, (two_c, two_c), 1)
    eye2 = (r2 == c2).astype(F32)
    decay_scale = float(np.exp(-0.5))

    def head_sum(x):
        parts = []
        for p in range(R_PAIRS):
            xp = x[:, p * LANES:(p + 1) * LANES]
            s0 = jnp.sum(jnp.where(lo, xp, 0.0), axis=-1, keepdims=True)
            s1 = jnp.sum(jnp.where(lo, 0.0, xp), axis=-1, keepdims=True)
            parts.append(jnp.where(lo, s0, s1))
        return jnp.concatenate(parts, axis=1)

    def stack_heads(x):
        return jnp.concatenate([jnp.where(lo, x, 0.0), jnp.where(lo, 0.0, x)], axis=0)

    pairs = range(R_PAIRS)
    pair = lambda x, p: x[:, p * LANES:(p + 1) * LANES]
    bf = lambda x: x.astype(BF16)
    row0 = lambda c: c * CHUNK if isinstance(c, int) else pl.multiple_of(c * CHUNK, CHUNK)

    def double(apow, tinv):
        both = [_bdot(apow[p], jnp.concatenate([bf(apow[p]), bf(tinv[p])], axis=1)) for p in pairs]
        return [x[:, :two_c] for x in both], [tinv[p] + both[p][:, two_c:] for p in pairs]

    def stage_a(c):
        r0 = row0(c)

        def shift_mix(x_ref, c_ref, mu_ref):
            x = x_ref[pl.ds(r0, CHUNK), :]
            prev = pltpu.roll(x, 1, axis=0)
            head = jnp.where(row_8 == 0, c_ref[SUBLANES - 1:SUBLANES, :], prev[:SUBLANES])
            prev = jnp.concatenate([head, prev[SUBLANES:]], axis=0)
            c_ref[...] = x[CHUNK - SUBLANES:CHUNK, :]
            return x + (prev - x) * mu_ref[...]

        xwa = shift_mix(wa_ref, cwa_ref, muwa_ref)
        xg = shift_mix(g_ref, cg_ref, mug_ref)
        y_w = _bdot(jnp.tanh(xwa), w2_ref[...])
        y_a = _bdot(xwa, a2_ref[...])
        gg = _bdot(_sigmoid(xg), g2_ref[...])
        yield
        rr = shift_mix(r_ref, cr_ref, mur_ref)
        kr = shift_mix(k_ref, ck_ref, muk_ref)
        vr = shift_mix(v_ref, cv_ref, muv_ref)
        kk = kr * kk_ref[...]
        kk = kk * lax.rsqrt(jnp.maximum(head_sum(kk * kk), 1e-24))
        yield
        lw = -decay_scale * _sigmoid(w0_ref[...] + y_w)
        aa = _sigmoid(a0_ref[...] + y_a)
        k2 = kr * (1.0 + (aa - 1.0) * ka_ref[...])
        cum = _select_dot(lower, lw)
        yield
        e_pos = jnp.exp(cum)
        e_neg = jnp.exp(-cum)
        alpha = -kk * jnp.exp(cum - lw)
        beta = kk * aa * e_neg
        kt = k2 * e_neg
        rb = rr * e_pos
        p_end = e_pos[CHUNK - 1:CHUNK, :]
        beta_end = beta * p_end
        kt_end = kt * p_end

        left = [jnp.concatenate([stack_heads(pair(alpha, p)), stack_heads(pair(rb, p))],
                                axis=0).astype(BF16) for p in pairs]
        right = [jnp.concatenate([pair(beta, p)] * 2 + [pair(kt, p)] * 2, axis=0).astype(BF16)
                 for p in pairs]
        v_st = [stack_heads(pair(vr, p)).astype(BF16) for p in pairs]
        prod = [_bdot_nt(left[p], right[p]) for p in pairs]
        yield
        end_st = [jnp.concatenate([stack_heads(pair(beta_end, p)), stack_heads(pair(kt_end, p))],
                                  axis=0).astype(BF16) for p in pairs]
        bonus = head_sum(rr * k2 * rk_ref[...]) * vr
        yield
        prod = [x.astype(BF16) * keep for x in prod]
        a_ab = [x[:two_c, :two_c] for x in prod]
        w_rb = [x[two_c:, :two_c] for x in prod]
        to_v = [x[:, two_c:] for x in prod]

        tinv = [eye2 + a.astype(F32) for a in a_ab]
        apow = [_bdot(a, a) for a in a_ab]
        on_v = [_bdot(to_v[p], v_st[p]) for p in pairs]
        yield
        for _ in range(DOUBLINGS_AHEAD):
            apow, tinv = double(apow, tinv)
            yield
        for p in pairs:
            h_left[p] = left[p]
            h_vst[p] = v_st[p]
            h_end[p] = end_st[p]
            h_wrb[p] = w_rb[p]
            h_onv[p] = on_v[p]
            h_apow[p] = bf(apow[p])
            h_tinv[p] = tinv[p]
        h_pend[...] = p_end
        h_bonus[...] = bonus
        h_gg[...] = gg

    def stage_b(c):
        r0 = row0(c)
        left = [h_left[p] for p in pairs]
        v_st = [h_vst[p] for p in pairs]
        end_st = [h_end[p] for p in pairs]
        w_rb = [h_wrb[p] for p in pairs]
        on_v = [h_onv[p] for p in pairs]
        apow = [h_apow[p] for p in pairs]
        tinv = [h_tinv[p] for p in pairs]
        p_end = h_pend[...]
        bonus = h_bonus[...]
        gg = h_gg[...]
        st = [st_ref[p] for p in pairs]
        on_s = [_bdot_nt(left[p], st[p]) for p in pairs]
        for _ in range(4 - DOUBLINGS_AHEAD):
            apow, tinv = double(apow, tinv)
            yield
        tinv = [tinv[p] + _bdot(apow[p], tinv[p]) for p in pairs]
        yield
        u = [_bdot(tinv[p], on_s[p][:two_c] + on_v[p][:two_c]) for p in pairs]
        yield
        o_st = [on_s[p][two_c:] + on_v[p][two_c:] + _bdot(w_rb[p], u[p]) for p in pairs]
        for p in pairs:
            uv = jnp.concatenate([bf(u[p]), v_st[p]], axis=0)
            st_ref[p] = pair(p_end, p) * st[p] + _bdot_tn(uv, end_st[p])
        yield
        o = jnp.concatenate([x[:CHUNK] + x[CHUNK:] for x in o_st], axis=1)
        mean = head_sum(o) * (1.0 / R_DH)
        oc = o - mean
        yield
        var = head_sum(oc * oc) * (1.0 / R_DH)
        y = oc * lax.rsqrt(var + R_LN_EPS) * lnw_ref[...] + lnb_ref[...] + bonus
        valid = (t * tt + r0 + row_i) >= PAD
        out_ref[pl.ds(r0, CHUNK), :] = jnp.where(valid, y * gg, 0.0).astype(BF16)

    def steady(c, carry):
        _run_stages(stage_b(c), stage_a(c + 1))
        return carry

    n_chunks = tt // CHUNK
    _run_stages(stage_a(0))
    lax.fori_loop(0, n_chunks - 1, steady, 0)
    _run_stages(stage_b(n_chunks - 1))


def _rwkv(z, vecs, mats, batch, lp):
    m = z.shape[0]
    tt = LONG_TILE if lp % LONG_TILE == 0 else SEQ_TILE
    nt = lp // tt
    row = lambda b, t: b * nt + t
    base = 4 * M_W
    two_c = 2 * CHUNK
    handoff = [pltpu.VMEM((R_PAIRS, 2 * two_c, LANES), BF16),
               pltpu.VMEM((R_PAIRS, two_c, LANES), BF16),
               pltpu.VMEM((R_PAIRS, 2 * two_c, LANES), BF16),
               pltpu.VMEM((R_PAIRS, two_c, two_c), BF16),
               pltpu.VMEM((R_PAIRS, 2 * two_c, LANES), F32),
               pltpu.VMEM((R_PAIRS, two_c, two_c), BF16),
               pltpu.VMEM((R_PAIRS, two_c, two_c), F32),
               pltpu.VMEM((1, R_W), F32),
               pltpu.VMEM((CHUNK, R_W), F32),
               pltpu.VMEM((CHUNK, R_W), F32)]
    in_specs = [
        pl.BlockSpec((tt, R_W), lambda b, t: (row(b, t), base // R_W)),
        pl.BlockSpec((tt, R_W), lambda b, t: (row(b, t), base // R_W + 1)),
        pl.BlockSpec((tt, R_W), lambda b, t: (row(b, t), base // R_W + 2)),
        pl.BlockSpec((tt, LANES), lambda b, t: (row(b, t), (base + 3 * R_W) // LANES)),
        pl.BlockSpec((tt, LANES), lambda b, t: (row(b, t), (base + 3 * R_W) // LANES + 1)),
    ]
    mu_r, mu_k, mu_v, mu_wa, mu_g, w0, a0, k_k, k_a, r_k, ln_w, ln_b = vecs
    w2, a2, g2 = mats
    args = [mu_r, mu_k, mu_v, mu_wa, mu_g, w0, w2, a0, a2, g2, k_k, k_a, r_k, ln_w, ln_b]
    in_specs += [_const_spec(a.shape) for a in args]
    return pl.pallas_call(
        _rwkv_kernel, grid=(batch, nt), in_specs=in_specs,
        out_specs=pl.BlockSpec((tt, R_W), lambda b, t: (row(b, t), 0)),
        out_shape=jax.ShapeDtypeStruct((m, R_W), BF16),
        scratch_shapes=[pltpu.VMEM((SUBLANES, R_W), F32), pltpu.VMEM((SUBLANES, R_W), F32),
                        pltpu.VMEM((SUBLANES, R_W), F32), pltpu.VMEM((SUBLANES, LANES), F32),
                        pltpu.VMEM((SUBLANES, LANES), F32),
                        pltpu.VMEM((R_PAIRS, LANES, LANES), F32)] + handoff,
        compiler_params=pltpu.CompilerParams(dimension_semantics=("arbitrary", "arbitrary"),
                                             vmem_limit_bytes=VMEM_LIMIT),
        name="rwkv7")(z, z, z, z, z, *args)


def _retention_kernel(q_ref, k_ref, v_ref, gate_ref, cos_ref, sin_ref, out_ref, st_ref):
    t = pl.program_id(1)
    c = q_ref.shape[0]

    @pl.when(t == 0)
    def _():
        st_ref[...] = jnp.zeros_like(st_ref)

    ri = lax.broadcasted_iota(jnp.int32, (c, c), 0)
    ci = lax.broadcasted_iota(jnp.int32, (c, c), 1)
    diff = (ri - ci).astype(F32)
    idx = lax.broadcasted_iota(jnp.int32, (c, 1), 0).astype(F32)
    cos = cos_ref[...]
    sin = sin_ref[...]
    half = T_DK // 2

    def rotary(x):
        swapped = jnp.concatenate([x[:, half:], x[:, :half]], axis=1)
        return x * cos + swapped * sin

    for h in range(T_HEADS):
        log_gamma = float(np.log(1.0 - 2.0 ** (-5.0 - h)))
        ks = slice(h * T_DK, (h + 1) * T_DK)
        vs = slice(h * T_DV, (h + 1) * T_DV)
        q = rotary(q_ref[:, ks])
        k = rotary(k_ref[:, ks]) * (T_DK ** -0.5)
        vb = v_ref[:, vs].astype(BF16)
        intra = jnp.where(diff >= 0, jnp.exp(log_gamma * jnp.maximum(diff, 0.0)), 0.0)
        q_dec = jnp.exp(log_gamma * (idx + 1.0))
        k_dec = jnp.exp(log_gamma * (c - 1.0 - idx))
        st = st_ref[h]
        s = _bdot_nt(q, k) * intra
        o = _bdot(s, vb) + _bdot(q * q_dec, st)
        st_ref[h] = float(np.exp(log_gamma * c)) * st + _bdot_tn(k * k_dec, vb)

        mu = jnp.mean(o, axis=-1, keepdims=True)
        oc = o - mu
        var = jnp.mean(oc * oc, axis=-1, keepdims=True)
        gate = gate_ref[:, vs]
        out_ref[:, vs] = (oc * lax.rsqrt(var + NORM_EPS) * (gate * _sigmoid(gate))).astype(BF16)


def _retention(z, cos, sin, batch, lp):
    m = z.shape[0]
    tt = SEQ_TILE
    nt = lp // tt
    row = lambda b, t: b * nt + t
    in_specs = [
        pl.BlockSpec((tt, D_MODEL), lambda b, t: (row(b, t), 0)),
        pl.BlockSpec((tt, D_MODEL), lambda b, t: (row(b, t), 1)),
        pl.BlockSpec((tt, T_WV), lambda b, t: (row(b, t), 1)),
        pl.BlockSpec((tt, T_WV), lambda b, t: (row(b, t), 2)),
        pl.BlockSpec((tt, T_DK), lambda b, t: (t, 0)),
        pl.BlockSpec((tt, T_DK), lambda b, t: (t, 0)),
    ]
    return pl.pallas_call(
        _retention_kernel, grid=(batch, nt), in_specs=in_specs,
        out_specs=pl.BlockSpec((tt, T_WV), lambda b, t: (row(b, t), 0)),
        out_shape=jax.ShapeDtypeStruct((m, T_WV), BF16),
        scratch_shapes=[pltpu.VMEM((T_HEADS, T_DK, T_DV), F32)],
        compiler_params=pltpu.CompilerParams(dimension_semantics=("arbitrary", "arbitrary"),
                                             vmem_limit_bytes=VMEM_LIMIT),
        name="retention")(z, z, z, z, cos, sin)


def _out_proj_kernel(n_in, *refs):
    h_ref = refs[0]
    a_refs = refs[1:1 + n_in]
    w_refs = refs[1 + n_in:1 + 2 * n_in]
    o_ref = refs[1 + 2 * n_in]
    acc = h_ref[...]
    for a_ref, w_ref in zip(a_refs, w_refs):
        acc = acc + jnp.dot(a_ref[...], w_ref[...], preferred_element_type=F32)
    o_ref[...] = acc


def _out_proj(h, acts, weights):
    m, d = h.shape
    tm = ROW_TILE
    in_specs = [pl.BlockSpec((tm, d), lambda i: (i, 0))]
    in_specs += [pl.BlockSpec((tm, a.shape[1]), lambda i: (i, 0)) for a in acts]
    in_specs += [_const_spec(w.shape) for w in weights]
    return pl.pallas_call(
        functools.partial(_out_proj_kernel, len(acts)), grid=(m // tm,), in_specs=in_specs,
        out_specs=pl.BlockSpec((tm, d), lambda i: (i, 0)),
        out_shape=jax.ShapeDtypeStruct((m, d), F32),
        compiler_params=pltpu.CompilerParams(dimension_semantics=("arbitrary",),
                                             vmem_limit_bytes=VMEM_LIMIT),
        name="out_proj")(h, *acts, *weights)


def _ffn_kernel(h_ref, g_ref, wv_ref, wg_ref, cw_ref, cb_ref, wd_ref, o_ref,
                carry_ref, hn_ref, act_ref):
    tm = h_ref.shape[0]

    @pl.when(pl.program_id(0) == 0)
    def _():
        carry_ref[...] = jnp.zeros_like(carry_ref)

    hn_ref[...] = _rmsnorm_bf16(h_ref[...], g_ref[...])
    row = lax.broadcasted_iota(jnp.int32, (SUBLANES, 1), 0)

    for c in range(N_FF_CHUNKS):
        cs = slice(c * FF_CHUNK, (c + 1) * FF_CHUNK)
        hn = hn_ref[...]
        val = jnp.dot(hn, wv_ref[:, cs], preferred_element_type=F32)
        gate = jnp.dot(hn, wg_ref[:, cs], preferred_element_type=F32)
        tail = carry_ref[:, cs]
        prev1 = pltpu.roll(gate, 1, axis=0)
        prev2 = pltpu.roll(gate, 2, axis=0)
        head1 = jnp.where(row == 0, tail[SUBLANES - 1:SUBLANES, :], prev1[:SUBLANES])
        head2 = jnp.where(row == 0, tail[SUBLANES - 2:SUBLANES - 1, :], prev2[:SUBLANES])
        head2 = jnp.where(row == 1, tail[SUBLANES - 1:SUBLANES, :], head2)
        prev1 = jnp.concatenate([head1, prev1[SUBLANES:]], axis=0)
        prev2 = jnp.concatenate([head2, prev2[SUBLANES:]], axis=0)
        carry_ref[:, cs] = gate[tm - SUBLANES:tm, :]
        conv = (cw_ref[2:3, cs] * gate + cw_ref[1:2, cs] * prev1 + cw_ref[0:1, cs] * prev2
                + cb_ref[:, cs])
        act_ref[:, cs] = (conv * _sigmoid(conv) * val).astype(BF16)
    o_ref[...] = h_ref[...] + jnp.dot(act_ref[...], wd_ref[...], preferred_element_type=F32)


def _ffn(h, g, wv, wg, cw, cb, wd):
    m, d = h.shape
    tm = ROW_TILE
    in_specs = [pl.BlockSpec((tm, d), lambda i: (i, 0)), _const_spec((1, d))]
    in_specs += [_const_spec(a.shape) for a in (wv, wg, cw, cb, wd)]
    return pl.pallas_call(
        _ffn_kernel, grid=(m // tm,), in_specs=in_specs,
        out_specs=pl.BlockSpec((tm, d), lambda i: (i, 0)),
        out_shape=jax.ShapeDtypeStruct((m, d), F32),
        scratch_shapes=[pltpu.VMEM((SUBLANES, D_FF), F32),
                        pltpu.VMEM((tm, d), BF16),
                        pltpu.VMEM((tm, D_FF), BF16)],
        compiler_params=pltpu.CompilerParams(dimension_semantics=("arbitrary",),
                                             vmem_limit_bytes=VMEM_LIMIT),
        name="conv_ffn")(h, g, wv, wg, cw, cb, wd)


def _final_norm_kernel(h_ref, g_ref, o_ref):
    x = h_ref[...]
    ms = jnp.mean(x * x, axis=-1, keepdims=True)
    o_ref[0] = x * lax.rsqrt(ms + NORM_EPS) * g_ref[...]


def _final_norm(h, g, batch, lp):
    d = h.shape[1]
    seq = lp - FRONT
    rows = 2 * ROW_TILE if seq % (2 * ROW_TILE) == 0 else FRONT
    return pl.pallas_call(
        _final_norm_kernel, grid=(batch, seq // rows),
        in_specs=[pl.BlockSpec((pl.Element(rows), pl.Element(d)),
                               lambda b, j: (pl.multiple_of(b * lp + FRONT + j * rows, FRONT), 0)),
                  _const_spec((1, d))],
        out_specs=pl.BlockSpec((1, rows, d), lambda b, j: (b, j, 0)),
        out_shape=jax.ShapeDtypeStruct((batch, seq, d), F32),
        compiler_params=pltpu.CompilerParams(dimension_semantics=("arbitrary", "arbitrary")),
        name="final_norm")(h, g)


def _ffn_weights(w_up, conv_w, conv_b, w_down):
    wv = w_up[:, :D_FF].astype(BF16)
    wg = w_up[:, D_FF:].astype(BF16)
    cw = jnp.pad(conv_w, ((0, SUBLANES - conv_w.shape[0]), (0, 0)))
    return wv, wg, cw, conv_b.reshape(1, D_FF), w_down.astype(BF16)


def kernel(x, meta_tokens, norm_mix, norm_ffn, norm_final, e_w_in, e_w_out, m_b_i, m_b_f, m_norm,
           r_mu, r_w0, r_w2, r_a0, r_a2, r_g2, r_k_k, r_k_a, r_r_k, r_ln_w, r_ln_b, o_w_in, o_w_out,
           f_w_up, f_conv_w, f_conv_b, f_w_down):
    batch, seq, d = x.shape
    lp = seq + FRONT
    assert d == D_MODEL and lp % SEQ_TILE == 0 and (batch * lp) % ROW_TILE == 0
    m = batch * lp
    row = lambda v: v.reshape(1, -1).astype(F32)

    meta = jnp.broadcast_to(meta_tokens[None].astype(x.dtype), (batch, N_META, d))
    h = jnp.concatenate([jnp.zeros((batch, PAD, d), x.dtype), meta, x], axis=1).reshape(m, d)

    w_in = e_w_in[0]
    n_m = 4 * M_W
    gates_w = w_in[:, n_m:n_m + 2 * M_HEADS]
    w0 = jnp.concatenate([w_in[:, :n_m], w_in[:, n_m + 2 * M_HEADS:], gates_w,
                          jnp.zeros((d, LANES - 2 * M_HEADS), F32)], axis=1).astype(BF16)
    z, gt = _norm_proj(h, row(norm_mix[0]), w0, gates_w.T.astype(BF16))
    gt_chunks = gt.reshape(SUBLANES, m // CHUNK, CHUNK).transpose(1, 0, 2)
    gate_bias = jnp.concatenate([m_b_i[0], m_b_f[0]])
    bias_col = jnp.pad(gate_bias, (0, LANES - 2 * M_HEADS)).reshape(1, LANES)
    bias_row = jnp.broadcast_to(gate_bias[:, None], (SUBLANES, CHUNK))
    mix_m = _mlstm(z, gt_chunks, bias_col, bias_row, row(m_norm[0]), batch, lp)

    mu = r_mu[0]
    rank_wa = r_w2.shape[1] + r_a2.shape[1]
    vecs = [row(mu[:R_W]), row(mu[R_W:2 * R_W]), row(mu[2 * R_W:3 * R_W]),
            row(mu[3 * R_W:3 * R_W + rank_wa]), row(mu[3 * R_W + rank_wa:]),
            row(r_w0[0]), row(r_a0[0]), row(r_k_k[0]), row(r_k_a[0]), row(r_r_k[0]),
            row(r_ln_w[0]), row(r_ln_b[0])]
    w2 = jnp.concatenate([r_w2[0], jnp.zeros_like(r_a2[0])], axis=0).astype(BF16)
    a2 = jnp.concatenate([jnp.zeros_like(r_w2[0]), r_a2[0]], axis=0).astype(BF16)
    mix_r = _rwkv(z, vecs, [w2, a2, r_g2[0].astype(BF16)], batch, lp)

    w_out = e_w_out[0].astype(BF16)
    h = _out_proj(h, [mix_m, mix_r], [w_out[:M_W], w_out[M_W:]])
    h = _ffn(h, row(norm_ffn[0]), *_ffn_weights(f_w_up[0], f_conv_w[0], f_conv_b[0], f_w_down[0]))

    w_in = o_w_in[0]
    perm = np.concatenate([np.arange(0, T_DK, 2), np.arange(1, T_DK, 2)])
    qk_cols = np.concatenate([hh * T_DK + perm for hh in range(T_HEADS)])
    w1 = jnp.concatenate([w_in[:, qk_cols], w_in[:, D_MODEL + qk_cols], w_in[:, 2 * D_MODEL:]],
                         axis=1).astype(BF16)
    z = _norm_proj(h, row(norm_mix[1]), w1)
    inv = 1.0 / (ROPE_BASE ** jnp.linspace(0.0, 1.0, T_DK // 2, dtype=F32))
    pos = jnp.arange(lp, dtype=F32) - PAD
    ang = pos[:, None] * inv[None, :]
    cos = jnp.concatenate([jnp.cos(ang), jnp.cos(ang)], axis=1)
    sin = jnp.concatenate([-jnp.sin(ang), jnp.sin(ang)], axis=1)
    o = _retention(z, cos, sin, batch, lp)
    h = _out_proj(h, [o], [o_w_out[0].astype(BF16)])
    h = _ffn(h, row(norm_ffn[1]), *_ffn_weights(f_w_up[1], f_conv_w[1], f_conv_b[1], f_w_down[1]))

    return _final_norm(h, row(norm_final), batch, lp)
```

```python
import functools

import numpy as np
import jax
import jax.numpy as jnp
from jax import lax
from jax.experimental import pallas as pl
from jax.experimental.pallas import tpu as pltpu

F32 = jnp.float32
BF16 = jnp.bfloat16
HIGHEST = lax.Precision.HIGHEST

D_MODEL = 1024
N_META = 16
NORM_EPS = 1e-6
M_HEADS = 4
M_DH = 128
M_W = M_HEADS * M_DH
GATE_CAP = 15.0
R_DH = 64
R_HEADS = 8
R_W = R_HEADS * R_DH
R_PAIRS = R_HEADS // 2
R_LN_EPS = 64e-5
T_HEADS = 4
T_DK = 256
T_DV = 512
T_WV = T_HEADS * T_DV
ROPE_BASE = 10000.0
D_FF = 2816
FF_CHUNK = 256
N_FF_CHUNKS = D_FF // FF_CHUNK

LANES = 128
SUBLANES = 8
FRONT = 128
PAD = FRONT - N_META
CHUNK = 64
M_CHUNK = 128
ROW_TILE = 512
SEQ_TILE = 384
LONG_TILE = 1408
DOUBLINGS_AHEAD = 1
VMEM_LIMIT = 56 * 1024 * 1024


def _const_spec(shape):
    nd = len(shape)
    return pl.BlockSpec(shape, lambda *_: (0,) * nd, pipeline_mode=pl.Buffered(1))


def _sigmoid(x):
    return 1.0 / (1.0 + jnp.exp(-x))


def _softplus(x):
    return jnp.maximum(x, 0.0) + jnp.log1p(jnp.exp(-jnp.abs(x)))


def _bdot(a, b):
    return jnp.dot(a.astype(BF16), b.astype(BF16), preferred_element_type=F32)


def _bdot_nt(a, b):
    return lax.dot_general(a.astype(BF16), b.astype(BF16), (((1,), (1,)), ((), ())),
                           preferred_element_type=F32)


def _bdot_tn(a, b):
    return lax.dot_general(a.astype(BF16), b.astype(BF16), (((0,), (0,)), ((), ())),
                           preferred_element_type=F32)


def _split3(x):
    hi = x.astype(BF16)
    rest = x - hi.astype(F32)
    mid = rest.astype(BF16)
    return hi, mid, (rest - mid.astype(F32)).astype(BF16)


def _select_dot(sel, x):
    hi, mid, low = _split3(x)
    dot = lambda y: jnp.dot(sel, y, preferred_element_type=F32)
    return dot(hi) + dot(mid) + dot(low)


def _dot_select(x, sel):
    hi, mid, low = _split3(x)
    dot = lambda y: jnp.dot(y, sel, preferred_element_type=F32)
    return dot(hi) + dot(mid) + dot(low)


def _run_stages(*stages):
    live = list(stages)
    while live:
        live = [s for s in live if next(s, live) is not live]


def _rmsnorm_bf16(x, g):
    ms = jnp.mean(x * x, axis=-1, keepdims=True)
    return (x * lax.rsqrt(ms + NORM_EPS) * g).astype(BF16)


def _col_chunks(n, width=512):
    return [(c, min(width, n - c)) for c in range(0, n, width)]


def _norm_proj_kernel(h_ref, g_ref, w_ref, o_ref, hn_ref):
    hn_ref[...] = _rmsnorm_bf16(h_ref[...], g_ref[...])
    for c0, cw in _col_chunks(w_ref.shape[1]):
        o_ref[:, c0:c0 + cw] = jnp.dot(hn_ref[...], w_ref[:, c0:c0 + cw],
                                       preferred_element_type=F32)


def _norm_proj_gates_kernel(h_ref, g_ref, w_ref, wgt_ref, o_ref, gt_ref, hn_ref):
    _norm_proj_kernel(h_ref, g_ref, w_ref, o_ref, hn_ref)
    gt_ref[...] = lax.dot_general(wgt_ref[...], hn_ref[...], (((1,), (1,)), ((), ())),
                                  preferred_element_type=F32)


def _norm_proj_split_kernel(h_ref, g_ref, w_ref, o_ref, ob_ref, hn_ref):
    hn_ref[...] = _rmsnorm_bf16(h_ref[...], g_ref[...])
    n_f32 = o_ref.shape[1]
    for c0, cw in _col_chunks(w_ref.shape[1]):
        z = jnp.dot(hn_ref[...], w_ref[:, c0:c0 + cw], preferred_element_type=F32)
        if c0 < n_f32:
            o_ref[:, c0:c0 + cw] = z
        else:
            ob_ref[:, c0 - n_f32:c0 - n_f32 + cw] = z.astype(BF16)


def _norm_proj(h, g, w, wgt=None, n_bf16=0):
    m, d = h.shape
    n = w.shape[1]
    tm = ROW_TILE
    in_specs = [pl.BlockSpec((tm, d), lambda i: (i, 0)), _const_spec((1, d)), _const_spec((d, n))]
    params = pltpu.CompilerParams(dimension_semantics=("arbitrary",), vmem_limit_bytes=VMEM_LIMIT)
    if wgt is None:
        n_f32 = n - n_bf16
        return pl.pallas_call(
            _norm_proj_split_kernel, grid=(m // tm,), in_specs=in_specs,
            out_specs=[pl.BlockSpec((tm, n_f32), lambda i: (i, 0)),
                       pl.BlockSpec((tm, n_bf16), lambda i: (i, 0))],
            out_shape=[jax.ShapeDtypeStruct((m, n_f32), F32),
                       jax.ShapeDtypeStruct((m, n_bf16), BF16)],
            scratch_shapes=[pltpu.VMEM((tm, d), BF16)],
            compiler_params=params, name="norm_proj")(h, g, w)
    return pl.pallas_call(
        _norm_proj_gates_kernel, grid=(m // tm,),
        in_specs=in_specs + [_const_spec(wgt.shape)],
        out_specs=[pl.BlockSpec((tm, n), lambda i: (i, 0)),
                   pl.BlockSpec((SUBLANES, tm), lambda i: (0, i))],
        out_shape=[jax.ShapeDtypeStruct((m, n), F32), jax.ShapeDtypeStruct((SUBLANES, m), F32)],
        scratch_shapes=[pltpu.VMEM((tm, d), BF16)],
        compiler_params=params, name="norm_proj_gates")(h, g, w, wgt)


def _gate_act(z, is_input_gate):
    capped = GATE_CAP * jnp.tanh(z / GATE_CAP)
    return jnp.where(is_input_gate, capped, -_softplus(-capped))


def _mlstm_kernel(q_ref, k_ref, v_ref, og_ref, gc_ref, gr_ref, bc_ref, br_ref, mn_ref,
                  out_ref, c_ref, m_ref, h_sv, h_kv, h_q, h_maxd, h_bcum, h_row, h_gate):
    t = pl.program_id(1)
    tt = q_ref.shape[0]

    @pl.when(t == 0)
    def _():
        c_ref[...] = jnp.zeros_like(c_ref)
        m_ref[...] = jnp.zeros_like(m_ref)

    cs = M_CHUNK
    ri = lax.broadcasted_iota(jnp.int32, (cs, cs), 0)
    ci = lax.broadcasted_iota(jnp.int32, (cs, cs), 1)
    causal = ci <= ri
    lower = jnp.where(causal, 1.0, 0.0).astype(BF16)
    upper = jnp.where(ri <= ci, 1.0, 0.0).astype(BF16)
    n_blocks = 2 * M_HEADS
    sel_r = lax.broadcasted_iota(jnp.int32, (LANES, n_blocks * LANES), 0)
    sel_c = lax.broadcasted_iota(jnp.int32, (LANES, n_blocks * LANES), 1)
    spread = jnp.where(sel_r == sel_c // LANES, 1.0, 0.0).astype(BF16)
    ones_blk = jnp.ones((cs, LANES), BF16)
    col_lane = lax.broadcasted_iota(jnp.int32, (1, LANES), 1)
    col_row = lax.broadcasted_iota(jnp.int32, (cs, 1), 0)
    row_sub = lax.broadcasted_iota(jnp.int32, (SUBLANES, 1), 0)
    row_lane = lax.broadcasted_iota(jnp.int32, (1, cs), 1)
    scale = M_DH ** -0.5
    heads = range(M_HEADS)
    hs = [slice(h * M_DH, (h + 1) * M_DH) for h in heads]
    blk = lambda x, j: x[:, j * LANES:(j + 1) * LANES]
    row0 = lambda c: c * cs if isinstance(c, int) else pl.multiple_of(c * cs, cs)
    finite = lambda x: jnp.where(x == -jnp.inf, 0.0, x)

    def stage_a(c):
        r0 = row0(c)
        pos0 = t * tt + r0
        gact = _gate_act(gc_ref[pl.ds(r0, cs), :] + bc_ref[...], col_lane < M_HEADS)
        spread_g = _dot_select(gact, spread)
        valid = pos0 + col_row >= PAD
        li_b = [jnp.where(valid, blk(spread_g, h), -jnp.inf) for h in heads]
        lf_b = jnp.where(valid, spread_g[:, M_HEADS * LANES:], 0.0)
        bcum_all = _select_dot(lower, lf_b)
        bcum_b = [blk(bcum_all, h) for h in heads]
        gr = _gate_act(gr_ref[c] + br_ref[...], row_sub < M_HEADS)
        gr = jnp.where(pos0 + row_lane >= PAD, gr, jnp.where(row_sub < M_HEADS, -jnp.inf, 0.0))
        bcum_r = _dot_select(jnp.where(row_sub < M_HEADS, 0.0, gr), upper)
        yield
        qb = [q_ref[pl.ds(r0, cs), hs[h]].astype(BF16) for h in heads]
        k = [k_ref[pl.ds(r0, cs), hs[h]] * scale for h in heads]
        vb = [jnp.concatenate([v_ref[pl.ds(r0, cs), hs[h]].astype(BF16), ones_blk], axis=1)
              for h in heads]
        qk = [_bdot_nt(qb[h], k[h]) for h in heads]
        yield
        brow = [bcum_r[M_HEADS + h:M_HEADS + h + 1, :] for h in heads]
        lirow = [gr[h:h + 1, :] for h in heads]
        g_b = [bcum_b[h][cs - 1:cs, :] for h in heads]
        dmat = [jnp.where(causal, bcum_b[h][:, :cs] - brow[h] + lirow[h], -jnp.inf)
                for h in heads]
        maxd = [jnp.max(dmat[h], axis=-1, keepdims=True) for h in heads]
        a_row = [g_b[h][:, :1] - brow[h] + lirow[h] for h in heads]
        amax = [jnp.max(a_row[h], axis=-1, keepdims=True) for h in heads]
        yield
        sp = [qk[h] * jnp.exp(dmat[h] - finite(maxd[h])) for h in heads]
        kwp = [k[h] * jnp.exp(g_b[h] - bcum_b[h] + li_b[h] - finite(amax[h])) for h in heads]
        svp = [_bdot(sp[h], vb[h]) for h in heads]
        yield
        kvp = [_bdot_tn(kwp[h], vb[h]) for h in heads]
        yield
        for h in heads:
            og = og_ref[pl.ds(r0, cs), hs[h]]
            h_sv[h] = svp[h]
            h_kv[h] = kvp[h]
            h_q[h] = qb[h]
            h_maxd[h] = jnp.broadcast_to(maxd[h], (cs, LANES))
            h_bcum[h] = bcum_b[h]
            h_row[h] = jnp.concatenate([jnp.broadcast_to(amax[h], (1, LANES)), g_b[h]], axis=0)
            h_gate[h] = mn_ref[:, hs[h]] * _sigmoid(og)

    def stage_b(c):
        r0 = row0(c)
        svp = [h_sv[h] for h in heads]
        kvp = [h_kv[h] for h in heads]
        qb = [h_q[h] for h in heads]
        maxd = [h_maxd[h] for h in heads]
        bcum_b = [h_bcum[h] for h in heads]
        amax = [h_row[h][0:1, :] for h in heads]
        g_b = [h_row[h][1:2, :] for h in heads]
        gate = [h_gate[h] for h in heads]
        c_st = [c_ref[h] for h in heads]
        m_st = [m_ref[h] for h in heads]
        qcn = [_bdot(qb[h], c_st[h]) for h in heads]
        inter = [bcum_b[h] + m_st[h] for h in heads]
        m_row = [jnp.maximum(inter[h], maxd[h]) for h in heads]
        yield
        m_new = [jnp.maximum(g_b[h] + m_st[h], amax[h]) for h in heads]
        for h in heads:
            decay = jnp.exp(g_b[h] + m_st[h] - m_new[h])[:, :1]
            grow = jnp.exp(finite(amax[h]) - m_new[h])[:, :1]
            c_ref[h] = decay * c_st[h] + grow * kvp[h]
            m_ref[h] = m_new[h]
        yield
        f_intra = [jnp.exp(finite(maxd[h]) - m_row[h]) for h in heads]
        w_inter = [jnp.exp(inter[h] - m_row[h]) for h in heads]
        yield
        for h in heads:
            num = f_intra[h] * svp[h][:, :M_DH] + w_inter[h] * qcn[h][:, :M_DH]
            den = f_intra[h] * svp[h][:, M_DH:] + w_inter[h] * qcn[h][:, M_DH:]
            hc = num / jnp.maximum(jnp.abs(den), jnp.exp(-m_row[h]))
            hn = hc * lax.rsqrt(jnp.mean(hc * hc, axis=-1, keepdims=True) + NORM_EPS)
            out_ref[pl.ds(r0, cs), hs[h]] = (hn * gate[h]).astype(BF16)

    def steady(c, carry):
        _run_stages(stage_b(c), stage_a(c + 1))
        return carry

    n_chunks = tt // cs
    _run_stages(stage_a(0))
    lax.fori_loop(0, n_chunks - 1, steady, 0)
    _run_stages(stage_b(n_chunks - 1))


def _mlstm(z, gt_chunks, bias_col, bias_row, m_norm, batch, lp):
    m = z.shape[0]
    tt = LONG_TILE if lp % LONG_TILE == 0 else SEQ_TILE
    nt = lp // tt
    row = lambda b, t: b * nt + t
    gate_col_block = 4 * M_W // LANES + (3 * R_W + 256) // LANES
    handoff = [pltpu.VMEM((M_HEADS, M_CHUNK, 2 * M_DH), F32),
               pltpu.VMEM((M_HEADS, M_DH, 2 * M_DH), F32),
               pltpu.VMEM((M_HEADS, M_CHUNK, M_DH), BF16),
               pltpu.VMEM((M_HEADS, M_CHUNK, LANES), F32),
               pltpu.VMEM((M_HEADS, M_CHUNK, LANES), F32),
               pltpu.VMEM((M_HEADS, 2, LANES), F32),
               pltpu.VMEM((M_HEADS, M_CHUNK, M_DH), F32)]
    in_specs = [
        pl.BlockSpec((tt, M_W), lambda b, t: (row(b, t), 0)),
        pl.BlockSpec((tt, M_W), lambda b, t: (row(b, t), 1)),
        pl.BlockSpec((tt, M_W), lambda b, t: (row(b, t), 2)),
        pl.BlockSpec((tt, M_W), lambda b, t: (row(b, t), 3)),
        pl.BlockSpec((tt, LANES), lambda b, t: (row(b, t), gate_col_block)),
        pl.BlockSpec((tt // M_CHUNK, SUBLANES, M_CHUNK), lambda b, t: (row(b, t), 0, 0)),
        _const_spec((1, LANES)), _const_spec((SUBLANES, M_CHUNK)), _const_spec((1, M_W)),
    ]
    return pl.pallas_call(
        _mlstm_kernel, grid=(batch, nt), in_specs=in_specs,
        out_specs=pl.BlockSpec((tt, M_W), lambda b, t: (row(b, t), 0)),
        out_shape=jax.ShapeDtypeStruct((m, M_W), BF16),
        scratch_shapes=[pltpu.VMEM((M_HEADS, M_DH, 2 * M_DH), F32),
                        pltpu.VMEM((M_HEADS, 1, LANES), F32)] + handoff,
        compiler_params=pltpu.CompilerParams(dimension_semantics=("arbitrary", "arbitrary"),
                                             vmem_limit_bytes=VMEM_LIMIT),
        name="mlstm")(z, z, z, z, z, gt_chunks, bias_col, bias_row, m_norm)


def _rwkv_kernel(r_ref, k_ref, v_ref, wa_ref, g_ref, mur_ref, muk_ref, muv_ref, muwa_ref, mug_ref,
                 w0_ref, w2_ref, a0_ref, a2_ref, g2_ref, kk_ref, ka_ref, rk_ref, lnw_ref, lnb_ref,
                 out_ref, cr_ref, ck_ref, cv_ref, cwa_ref, cg_ref, st_ref,
                 h_left, h_vst, h_end, h_wrb, h_onv, h_apow, h_tinv, h_pend, h_bonus, h_gg):
    t = pl.program_id(1)
    tt = r_ref.shape[0]
    two_c = 2 * CHUNK

    @pl.when(t == 0)
    def _():
        for ref in (cr_ref, ck_ref, cv_ref, cwa_ref, cg_ref, st_ref):
            ref[...] = jnp.zeros_like(ref)

    row_i = lax.broadcasted_iota(jnp.int32, (CHUNK, 1), 0)
    row_8 = lax.broadcasted_iota(jnp.int32, (SUBLANES, 1), 0)
    lo = lax.broadcasted_iota(jnp.int32, (1, LANES), 1) < R_DH
    ri = lax.broadcasted_iota(jnp.int32, (CHUNK, CHUNK), 0)
    ci = lax.broadcasted_iota(jnp.int32, (CHUNK, CHUNK), 1)
    lower = (ci <= ri).astype(BF16)
    r4 = lax.broadcasted_iota(jnp.int32, (4 * CHUNK, 4 * CHUNK), 0)
    c4 = lax.broadcasted_iota(jnp.int32, (4 * CHUNK, 4 * CHUNK), 1)
    same_head = ((r4 // CHUNK) % 2) == ((c4 // CHUNK) % 2)
    below = (c4 % CHUNK < r4 % CHUNK) | ((r4 >= two_c) & (c4 % CHUNK == r4 % CHUNK))
    keep = jnp.where(same_head & below, 1.0, 0.0).astype(BF16)
    r2 = lax.broadcasted_iota(jnp.int32, (two_c, two_c), 0)
    c2 = lax.broadcasted_iota(jnp.int32, (two_c, two_c), 1)
    eye2 = (r2 == c2).astype(F32)
    decay_scale = float(np.exp(-0.5))

    def head_sum(x):
        parts = []
        for p in range(R_PAIRS):
            xp = x[:, p * LANES:(p + 1) * LANES]
            s0 = jnp.sum(jnp.where(lo, xp, 0.0), axis=-1, keepdims=True)
            s1 = jnp.sum(jnp.where(lo, 0.0, xp), axis=-1, keepdims=True)
            parts.append(jnp.where(lo, s0, s1))
        return jnp.concatenate(parts, axis=1)

    def stack_heads(x):
        return jnp.concatenate([jnp.where(lo, x, 0.0), jnp.where(lo, 0.0, x)], axis=0)

    pairs = range(R_PAIRS)
    pair = lambda x, p: x[:, p * LANES:(p + 1) * LANES]
    bf = lambda x: x.astype(BF16)
    row0 = lambda c: c * CHUNK if isinstance(c, int) else pl.multiple_of(c * CHUNK, CHUNK)

    def double(apow, tinv):
        both = [_bdot(apow[p], jnp.concatenate([bf(apow[p]), bf(tinv[p])], axis=1)) for p in pairs]
        return [x[:, :two_c] for x in both], [tinv[p] + both[p][:, two_c:] for p in pairs]

    def stage_a(c):
        r0 = row0(c)

        def shift_mix(x_ref, c_ref, mu_ref):
            x = x_ref[pl.ds(r0, CHUNK), :]
            prev = pltpu.roll(x, 1, axis=0)
            head = jnp.where(row_8 == 0, c_ref[SUBLANES - 1:SUBLANES, :], prev[:SUBLANES])
            prev = jnp.concatenate([head, prev[SUBLANES:]], axis=0)
            c_ref[...] = x[CHUNK - SUBLANES:CHUNK, :]
            return x + (prev - x) * mu_ref[...]

        xwa = shift_mix(wa_ref, cwa_ref, muwa_ref)
        xg = shift_mix(g_ref, cg_ref, mug_ref)
        y_w = _bdot(jnp.tanh(xwa), w2_ref[...])
        y_a = _bdot(xwa, a2_ref[...])
        gg = _bdot(_sigmoid(xg), g2_ref[...])
        yield
        rr = shift_mix(r_ref, cr_ref, mur_ref)
        kr = shift_mix(k_ref, ck_ref, muk_ref)
        vr = shift_mix(v_ref, cv_ref, muv_ref)
        kk = kr * kk_ref[...]
        kk = kk * lax.rsqrt(jnp.maximum(head_sum(kk * kk), 1e-24))
        yield
        lw = -decay_scale * _sigmoid(w0_ref[...] + y_w)
        aa = _sigmoid(a0_ref[...] + y_a)
        k2 = kr * (1.0 + (aa - 1.0) * ka_ref[...])
        cum = _select_dot(lower, lw)
        yield
        e_pos = jnp.exp(cum)
        e_neg = jnp.exp(-cum)
        alpha = -kk * jnp.exp(cum - lw)
        beta = kk * aa * e_neg
        kt = k2 * e_neg
        rb = rr * e_pos
        p_end = e_pos[CHUNK - 1:CHUNK, :]
        beta_end = beta * p_end
        kt_end = kt * p_end

        left = [jnp.concatenate([stack_heads(pair(alpha, p)), stack_heads(pair(rb, p))],
                                axis=0).astype(BF16) for p in pairs]
        right = [jnp.concatenate([pair(beta, p)] * 2 + [pair(kt, p)] * 2, axis=0).astype(BF16)
                 for p in pairs]
        v_st = [stack_heads(pair(vr, p)).astype(BF16) for p in pairs]
        prod = [_bdot_nt(left[p], right[p]) for p in pairs]
        yield
        end_st = [jnp.concatenate([stack_heads(pair(beta_end, p)), stack_heads(pair(kt_end, p))],
                                  axis=0).astype(BF16) for p in pairs]
        bonus = head_sum(rr * k2 * rk_ref[...]) * vr
        yield
        prod = [x.astype(BF16) * keep for x in prod]
        a_ab = [x[:two_c, :two_c] for x in prod]
        w_rb = [x[two_c:, :two_c] for x in prod]
        to_v = [x[:, two_c:] for x in prod]

        tinv = [eye2 + a.astype(F32) for a in a_ab]
        apow = [_bdot(a, a) for a in a_ab]
        on_v = [_bdot(to_v[p], v_st[p]) for p in pairs]
        yield
        for _ in range(DOUBLINGS_AHEAD):
            apow, tinv = double(apow, tinv)
            yield
        for p in pairs:
            h_left[p] = left[p]
            h_vst[p] = v_st[p]
            h_end[p] = end_st[p]
            h_wrb[p] = w_rb[p]
            h_onv[p] = on_v[p]
            h_apow[p] = bf(apow[p])
            h_tinv[p] = tinv[p]
        h_pend[...] = p_end
        h_bonus[...] = bonus
        h_gg[...] = gg

    def stage_b(c):
        r0 = row0(c)
        left = [h_left[p] for p in pairs]
        v_st = [h_vst[p] for p in pairs]
        end_st = [h_end[p] for p in pairs]
        w_rb = [h_wrb[p] for p in pairs]
        on_v = [h_onv[p] for p in pairs]
        apow = [h_apow[p] for p in pairs]
        tinv = [h_tinv[p] for p in pairs]
        p_end = h_pend[...]
        bonus = h_bonus[...]
        gg = h_gg[...]
        st = [st_ref[p] for p in pairs]
        on_s = [_bdot_nt(left[p], st[p]) for p in pairs]
        for _ in range(4 - DOUBLINGS_AHEAD):
            apow, tinv = double(apow, tinv)
            yield
        tinv = [tinv[p] + _bdot(apow[p], tinv[p]) for p in pairs]
        yield
        u = [_bdot(tinv[p], on_s[p][:two_c] + on_v[p][:two_c]) for p in pairs]
        yield
        o_st = [on_s[p][two_c:] + on_v[p][two_c:] + _bdot(w_rb[p], u[p]) for p in pairs]
        for p in pairs:
            uv = jnp.concatenate([bf(u[p]), v_st[p]], axis=0)
            st_ref[p] = pair(p_end, p) * st[p] + _bdot_tn(uv, end_st[p])
        yield
        o = jnp.concatenate([x[:CHUNK] + x[CHUNK:] for x in o_st], axis=1)
        mean = head_sum(o) * (1.0 / R_DH)
        oc = o - mean
        yield
        var = head_sum(oc * oc) * (1.0 / R_DH)
        y = oc * lax.rsqrt(var + R_LN_EPS) * lnw_ref[...] + lnb_ref[...] + bonus
        valid = (t * tt + r0 + row_i) >= PAD
        out_ref[pl.ds(r0, CHUNK), :] = jnp.where(valid, y * gg, 0.0).astype(BF16)

    def steady(c, carry):
        _run_stages(stage_b(c), stage_a(c + 1))
        return carry

    n_chunks = tt // CHUNK
    _run_stages(stage_a(0))
    lax.fori_loop(0, n_chunks - 1, steady, 0)
    _run_stages(stage_b(n_chunks - 1))


def _rwkv(z, vecs, mats, batch, lp):
    m = z.shape[0]
    tt = LONG_TILE if lp % LONG_TILE == 0 else SEQ_TILE
    nt = lp // tt
    row = lambda b, t: b * nt + t
    base = 4 * M_W
    two_c = 2 * CHUNK
    handoff = [pltpu.VMEM((R_PAIRS, 2 * two_c, LANES), BF16),
               pltpu.VMEM((R_PAIRS, two_c, LANES), BF16),
               pltpu.VMEM((R_PAIRS, 2 * two_c, LANES), BF16),
               pltpu.VMEM((R_PAIRS, two_c, two_c), BF16),
               pltpu.VMEM((R_PAIRS, 2 * two_c, LANES), F32),
               pltpu.VMEM((R_PAIRS, two_c, two_c), BF16),
               pltpu.VMEM((R_PAIRS, two_c, two_c), F32),
               pltpu.VMEM((1, R_W), F32),
               pltpu.VMEM((CHUNK, R_W), F32),
               pltpu.VMEM((CHUNK, R_W), F32)]
    in_specs = [
        pl.BlockSpec((tt, R_W), lambda b, t: (row(b, t), base // R_W)),
        pl.BlockSpec((tt, R_W), lambda b, t: (row(b, t), base // R_W + 1)),
        pl.BlockSpec((tt, R_W), lambda b, t: (row(b, t), base // R_W + 2)),
        pl.BlockSpec((tt, LANES), lambda b, t: (row(b, t), (base + 3 * R_W) // LANES)),
        pl.BlockSpec((tt, LANES), lambda b, t: (row(b, t), (base + 3 * R_W) // LANES + 1)),
    ]
    mu_r, mu_k, mu_v, mu_wa, mu_g, w0, a0, k_k, k_a, r_k, ln_w, ln_b = vecs
    w2, a2, g2 = mats
    args = [mu_r, mu_k, mu_v, mu_wa, mu_g, w0, w2, a0, a2, g2, k_k, k_a, r_k, ln_w, ln_b]
    in_specs += [_const_spec(a.shape) for a in args]
    return pl.pallas_call(
        _rwkv_kernel, grid=(batch, nt), in_specs=in_specs,
        out_specs=pl.BlockSpec((tt, R_W), lambda b, t: (row(b, t), 0)),
        out_shape=jax.ShapeDtypeStruct((m, R_W), BF16),
        scratch_shapes=[pltpu.VMEM((SUBLANES, R_W), F32), pltpu.VMEM((SUBLANES, R_W), F32),
                        pltpu.VMEM((SUBLANES, R_W), F32), pltpu.VMEM((SUBLANES, LANES), F32),
                        pltpu.VMEM((SUBLANES, LANES), F32),
                        pltpu.VMEM((R_PAIRS, LANES, LANES), F32)] + handoff,
        compiler_params=pltpu.CompilerParams(dimension_semantics=("arbitrary", "arbitrary"),
                                             vmem_limit_bytes=VMEM_LIMIT),
        name="rwkv7")(z, z, z, z, z, *args)


def _retention_kernel(q_ref, k_ref, gate_ref, v_ref, cos_ref, sin_ref, out_ref, st_ref):
    t = pl.program_id(1)
    c = q_ref.shape[0]

    @pl.when(t == 0)
    def _():
        st_ref[...] = jnp.zeros_like(st_ref)

    ri = lax.broadcasted_iota(jnp.int32, (c, c), 0)
    ci = lax.broadcasted_iota(jnp.int32, (c, c), 1)
    causal = jnp.where(ci <= ri, 1.0, 0.0).astype(BF16)
    steps = lax.broadcasted_iota(jnp.int32, (c, 1), 0).astype(F32) + 1.0
    cos = cos_ref[...]
    sin = sin_ref[...]
    half = T_DK // 2

    def rotary(x):
        swapped = jnp.concatenate([x[:, half:], x[:, :half]], axis=1)
        return x * cos + swapped * sin

    for h in range(T_HEADS):
        log_gamma = float(np.log(1.0 - 2.0 ** (-5.0 - h)))
        ks = slice(h * T_DK, (h + 1) * T_DK)
        vs = slice(h * T_DV, (h + 1) * T_DV)
        q = (rotary(q_ref[:, ks]) * jnp.exp(log_gamma * steps)).astype(BF16)
        k = (rotary(k_ref[:, ks]) * (jnp.exp(-log_gamma * steps) * (T_DK ** -0.5))).astype(BF16)
        vb = v_ref[:, vs]
        st = st_ref[h]
        s = _bdot_nt(q, k).astype(BF16) * causal
        o = _bdot(s, vb) + _bdot(q, st)
        st_ref[h] = float(np.exp(log_gamma * c)) * (st + _bdot_tn(k, vb))

        mu = jnp.mean(o, axis=-1, keepdims=True)
        oc = o - mu
        var = jnp.mean(oc * oc, axis=-1, keepdims=True)
        gate = gate_ref[:, vs]
        out_ref[:, vs] = (oc * lax.rsqrt(var + NORM_EPS) * (gate * _sigmoid(gate))).astype(BF16)


def _retention(z, v, cos, sin, batch, lp):
    m = z.shape[0]
    tt = SEQ_TILE
    nt = lp // tt
    row = lambda b, t: b * nt + t
    in_specs = [
        pl.BlockSpec((tt, D_MODEL), lambda b, t: (row(b, t), 0)),
        pl.BlockSpec((tt, D_MODEL), lambda b, t: (row(b, t), 1)),
        pl.BlockSpec((tt, T_WV), lambda b, t: (row(b, t), 1)),
        pl.BlockSpec((tt, T_WV), lambda b, t: (row(b, t), 0)),
        pl.BlockSpec((tt, T_DK), lambda b, t: (t, 0)),
        pl.BlockSpec((tt, T_DK), lambda b, t: (t, 0)),
    ]
    return pl.pallas_call(
        _retention_kernel, grid=(batch, nt), in_specs=in_specs,
        out_specs=pl.BlockSpec((tt, T_WV), lambda b, t: (row(b, t), 0)),
        out_shape=jax.ShapeDtypeStruct((m, T_WV), BF16),
        scratch_shapes=[pltpu.VMEM((T_HEADS, T_DK, T_DV), F32)],
        compiler_params=pltpu.CompilerParams(dimension_semantics=("arbitrary", "arbitrary"),
                                             vmem_limit_bytes=VMEM_LIMIT),
        name="retention")(z, z, z, v, cos, sin)


def _ffn_kernel(n_mix, *refs):
    h_ref = refs[0]
    a_refs = refs[1:1 + n_mix]
    wo_refs = refs[1 + n_mix:1 + 2 * n_mix]
    g_ref, wv_ref, wg_ref, cw_ref, cb_ref, wd_ref, o_ref, carry_ref, hn_ref, act_ref = (
        refs[1 + 2 * n_mix:])
    tm = h_ref.shape[0]

    @pl.when(pl.program_id(0) == 0)
    def _():
        carry_ref[...] = jnp.zeros_like(carry_ref)

    mixed = h_ref[...]
    for a_ref, wo_ref in zip(a_refs, wo_refs):
        mixed = mixed + jnp.dot(a_ref[...], wo_ref[...], preferred_element_type=F32)
    o_ref[...] = mixed
    hn_ref[...] = _rmsnorm_bf16(mixed, g_ref[...])
    row = lax.broadcasted_iota(jnp.int32, (SUBLANES, 1), 0)

    for c in range(N_FF_CHUNKS):
        cs = slice(c * FF_CHUNK, (c + 1) * FF_CHUNK)
        hn = hn_ref[...]
        val = jnp.dot(hn, wv_ref[:, cs], preferred_element_type=F32)
        gate = jnp.dot(hn, wg_ref[:, cs], preferred_element_type=F32)
        tail = carry_ref[:, cs]
        prev1 = pltpu.roll(gate, 1, axis=0)
        prev2 = pltpu.roll(gate, 2, axis=0)
        head1 = jnp.where(row == 0, tail[SUBLANES - 1:SUBLANES, :], prev1[:SUBLANES])
        head2 = jnp.where(row == 0, tail[SUBLANES - 2:SUBLANES - 1, :], prev2[:SUBLANES])
        head2 = jnp.where(row == 1, tail[SUBLANES - 1:SUBLANES, :], head2)
        prev1 = jnp.concatenate([head1, prev1[SUBLANES:]], axis=0)
        prev2 = jnp.concatenate([head2, prev2[SUBLANES:]], axis=0)
        carry_ref[:, cs] = gate[tm - SUBLANES:tm, :]
        conv = (cw_ref[2:3, cs] * gate + cw_ref[1:2, cs] * prev1 + cw_ref[0:1, cs] * prev2
                + cb_ref[:, cs])
        act_ref[:, cs] = (conv * _sigmoid(conv) * val).astype(BF16)
    o_ref[...] += jnp.dot(act_ref[...], wd_ref[...], preferred_element_type=F32)


def _mix_ffn(h, acts, w_outs, g, wv, wg, cw, cb, wd):
    m, d = h.shape
    tm = ROW_TILE
    in_specs = [pl.BlockSpec((tm, d), lambda i: (i, 0))]
    in_specs += [pl.BlockSpec((tm, a.shape[1]), lambda i: (i, 0)) for a in acts]
    in_specs += [_const_spec(a.shape) for a in (*w_outs, g, wv, wg, cw, cb, wd)]
    return pl.pallas_call(
        functools.partial(_ffn_kernel, len(acts)), grid=(m // tm,), in_specs=in_specs,
        out_specs=pl.BlockSpec((tm, d), lambda i: (i, 0)),
        out_shape=jax.ShapeDtypeStruct((m, d), F32),
        scratch_shapes=[pltpu.VMEM((SUBLANES, D_FF), F32),
                        pltpu.VMEM((tm, d), BF16),
                        pltpu.VMEM((tm, D_FF), BF16)],
        compiler_params=pltpu.CompilerParams(dimension_semantics=("arbitrary",),
                                             vmem_limit_bytes=VMEM_LIMIT),
        name="mix_ffn")(h, *acts, *w_outs, g, wv, wg, cw, cb, wd)


def _final_norm_kernel(h_ref, g_ref, o_ref):
    x = h_ref[...]
    ms = jnp.mean(x * x, axis=-1, keepdims=True)
    o_ref[0] = x * lax.rsqrt(ms + NORM_EPS) * g_ref[...]


def _final_norm(h, g, batch, lp):
    d = h.shape[1]
    seq = lp - FRONT
    rows = 2 * ROW_TILE if seq % (2 * ROW_TILE) == 0 else FRONT
    return pl.pallas_call(
        _final_norm_kernel, grid=(batch, seq // rows),
        in_specs=[pl.BlockSpec((pl.Element(rows), pl.Element(d)),
                               lambda b, j: (pl.multiple_of(b * lp + FRONT + j * rows, FRONT), 0)),
                  _const_spec((1, d))],
        out_specs=pl.BlockSpec((1, rows, d), lambda b, j: (b, j, 0)),
        out_shape=jax.ShapeDtypeStruct((batch, seq, d), F32),
        compiler_params=pltpu.CompilerParams(dimension_semantics=("arbitrary", "arbitrary")),
        name="final_norm")(h, g)


def _ffn_weights(w_up, conv_w, conv_b, w_down):
    wv = w_up[:, :D_FF].astype(BF16)
    wg = w_up[:, D_FF:].astype(BF16)
    cw = jnp.pad(conv_w, ((0, SUBLANES - conv_w.shape[0]), (0, 0)))
    return wv, wg, cw, conv_b.reshape(1, D_FF), w_down.astype(BF16)


def kernel(x, meta_tokens, norm_mix, norm_ffn, norm_final, e_w_in, e_w_out, m_b_i, m_b_f, m_norm,
           r_mu, r_w0, r_w2, r_a0, r_a2, r_g2, r_k_k, r_k_a, r_r_k, r_ln_w, r_ln_b, o_w_in, o_w_out,
           f_w_up, f_conv_w, f_conv_b, f_w_down):
    batch, seq, d = x.shape
    lp = seq + FRONT
    assert d == D_MODEL and lp % SEQ_TILE == 0 and (batch * lp) % ROW_TILE == 0
    m = batch * lp
    row = lambda v: v.reshape(1, -1).astype(F32)

    meta = jnp.broadcast_to(meta_tokens[None].astype(x.dtype), (batch, N_META, d))
    h = jnp.concatenate([jnp.zeros((batch, PAD, d), x.dtype), meta, x], axis=1).reshape(m, d)

    w_in = e_w_in[0]
    n_m = 4 * M_W
    gates_w = w_in[:, n_m:n_m + 2 * M_HEADS]
    w0 = jnp.concatenate([w_in[:, :n_m], w_in[:, n_m + 2 * M_HEADS:], gates_w,
                          jnp.zeros((d, LANES - 2 * M_HEADS), F32)], axis=1).astype(BF16)
    z, gt = _norm_proj(h, row(norm_mix[0]), w0, gates_w.T.astype(BF16))
    gt_chunks = gt.reshape(SUBLANES, m // M_CHUNK, M_CHUNK).transpose(1, 0, 2)
    gate_bias = jnp.concatenate([m_b_i[0], m_b_f[0]])
    bias_col = jnp.pad(gate_bias, (0, LANES - 2 * M_HEADS)).reshape(1, LANES)
    bias_row = jnp.broadcast_to(gate_bias[:, None], (SUBLANES, M_CHUNK))
    mix_m = _mlstm(z, gt_chunks, bias_col, bias_row, row(m_norm[0]), batch, lp)

    mu = r_mu[0]
    rank_wa = r_w2.shape[1] + r_a2.shape[1]
    vecs = [row(mu[:R_W]), row(mu[R_W:2 * R_W]), row(mu[2 * R_W:3 * R_W]),
            row(mu[3 * R_W:3 * R_W + rank_wa]), row(mu[3 * R_W + rank_wa:]),
            row(r_w0[0]), row(r_a0[0]), row(r_k_k[0]), row(r_k_a[0]), row(r_r_k[0]),
            row(r_ln_w[0]), row(r_ln_b[0])]
    w2 = jnp.concatenate([r_w2[0], jnp.zeros_like(r_a2[0])], axis=0).astype(BF16)
    a2 = jnp.concatenate([jnp.zeros_like(r_w2[0]), r_a2[0]], axis=0).astype(BF16)
    mix_r = _rwkv(z, vecs, [w2, a2, r_g2[0].astype(BF16)], batch, lp)

    w_out = e_w_out[0].astype(BF16)
    h = _mix_ffn(h, [mix_m, mix_r], [w_out[:M_W], w_out[M_W:]], row(norm_ffn[0]),
                 *_ffn_weights(f_w_up[0], f_conv_w[0], f_conv_b[0], f_w_down[0]))

    w_in = o_w_in[0]
    perm = np.concatenate([np.arange(0, T_DK, 2), np.arange(1, T_DK, 2)])
    qk_cols = np.concatenate([hh * T_DK + perm for hh in range(T_HEADS)])
    w1 = jnp.concatenate([w_in[:, qk_cols], w_in[:, D_MODEL + qk_cols],
                          w_in[:, 2 * D_MODEL + T_WV:], w_in[:, 2 * D_MODEL:2 * D_MODEL + T_WV]],
                         axis=1).astype(BF16)
    z, v = _norm_proj(h, row(norm_mix[1]), w1, n_bf16=T_WV)
    inv = 1.0 / (ROPE_BASE ** jnp.linspace(0.0, 1.0, T_DK // 2, dtype=F32))
    pos = jnp.arange(lp, dtype=F32) - PAD
    ang = pos[:, None] * inv[None, :]
    cos = jnp.concatenate([jnp.cos(ang), jnp.cos(ang)], axis=1)
    sin = jnp.concatenate([-jnp.sin(ang), jnp.sin(ang)], axis=1)
    o = _retention(z, v, cos, sin, batch, lp)
    h = _mix_ffn(h, [o], [o_w_out[0].astype(BF16)], row(norm_ffn[1]),
                 *_ffn_weights(f_w_up[1], f_conv_w[1], f_conv_b[1], f_w_down[1]))

    return _final_norm(h, row(norm_final), batch, lp)
```

```python
import functools

import numpy as np
import jax
import jax.numpy as jnp
from jax import lax
from jax.experimental import pallas as pl
from jax.experimental.pallas import tpu as pltpu

F32 = jnp.float32
BF16 = jnp.bfloat16
HIGHEST = lax.Precision.HIGHEST

D_MODEL = 1024
N_META = 16
NORM_EPS = 1e-6
M_HEADS = 4
M_DH = 128
M_W = M_HEADS * M_DH
GATE_CAP = 15.0
R_DH = 64
R_HEADS = 8
R_W = R_HEADS * R_DH
R_PAIRS = R_HEADS // 2
R_LN_EPS = 64e-5
T_HEADS = 4
T_DK = 256
T_DV = 512
T_WV = T_HEADS * T_DV
ROPE_BASE = 10000.0
D_FF = 2816
FF_CHUNK = 256
N_FF_CHUNKS = D_FF // FF_CHUNK

LANES = 128
SUBLANES = 8
FRONT = 128
PAD = FRONT - N_META
CHUNK = 64
M_CHUNK = 128
ROW_TILE = 512
SEQ_TILE = 384
LONG_TILE = 1408
DOUBLINGS_AHEAD = 1
A_STEPS_PER_B_STEP = 4
VMEM_LIMIT = 56 * 1024 * 1024


def _const_spec(shape):
    nd = len(shape)
    return pl.BlockSpec(shape, lambda *_: (0,) * nd, pipeline_mode=pl.Buffered(1))


def _sigmoid(x):
    return 1.0 / (1.0 + jnp.exp(-x))


def _softplus(x):
    return jnp.maximum(x, 0.0) + jnp.log1p(jnp.exp(-jnp.abs(x)))


def _bdot(a, b):
    return jnp.dot(a.astype(BF16), b.astype(BF16), preferred_element_type=F32)


def _bdot_nt(a, b):
    return lax.dot_general(a.astype(BF16), b.astype(BF16), (((1,), (1,)), ((), ())),
                           preferred_element_type=F32)


def _bdot_tn(a, b):
    return lax.dot_general(a.astype(BF16), b.astype(BF16), (((0,), (0,)), ((), ())),
                           preferred_element_type=F32)


def _split3(x):
    hi = x.astype(BF16)
    rest = x - hi.astype(F32)
    mid = rest.astype(BF16)
    return hi, mid, (rest - mid.astype(F32)).astype(BF16)


def _select_dot(sel, x):
    hi, mid, low = _split3(x)
    dot = lambda y: jnp.dot(sel, y, preferred_element_type=F32)
    return dot(hi) + dot(mid) + dot(low)


def _dot_select(x, sel):
    hi, mid, low = _split3(x)
    dot = lambda y: jnp.dot(y, sel, preferred_element_type=F32)
    return dot(hi) + dot(mid) + dot(low)


def _run_stages(*stages, steps=None):
    steps = steps or (1,) * len(stages)
    live = list(zip(stages, steps))
    done = object()
    while live:
        live = [(s, n) for s, n in live if all(next(s, done) is not done for _ in range(n))]


def _rmsnorm_bf16(x, g):
    ms = jnp.mean(x * x, axis=-1, keepdims=True)
    return (x * lax.rsqrt(ms + NORM_EPS) * g).astype(BF16)


def _col_chunks(n, width=512):
    return [(c, min(width, n - c)) for c in range(0, n, width)]


def _norm_proj_kernel(h_ref, g_ref, w_ref, o_ref, hn_ref):
    hn_ref[...] = _rmsnorm_bf16(h_ref[...], g_ref[...])
    for c0, cw in _col_chunks(w_ref.shape[1]):
        o_ref[:, c0:c0 + cw] = jnp.dot(hn_ref[...], w_ref[:, c0:c0 + cw],
                                       preferred_element_type=F32)


def _norm_proj_gates_kernel(h_ref, g_ref, w_ref, wgt_ref, o_ref, gt_ref, hn_ref):
    _norm_proj_kernel(h_ref, g_ref, w_ref, o_ref, hn_ref)
    gt_ref[...] = lax.dot_general(wgt_ref[...], hn_ref[...], (((1,), (1,)), ((), ())),
                                  preferred_element_type=F32)


def _norm_proj_retention_kernel(h_ref, g_ref, w_ref, cos_ref, sin_ref, o_ref, ob_ref, hn_ref):
    hn_ref[...] = _rmsnorm_bf16(h_ref[...], g_ref[...])
    n_f32 = o_ref.shape[1]
    half = T_DK // 2
    for c0, cw in _col_chunks(w_ref.shape[1], 2 * T_DK):
        z = jnp.dot(hn_ref[...], w_ref[:, c0:c0 + cw], preferred_element_type=F32)
        if c0 < 2 * D_MODEL:
            cos = cos_ref[...]
            sin = sin_ref[...]
            for j in range(0, cw, T_DK):
                x = z[:, j:j + T_DK]
                swapped = jnp.concatenate([x[:, half:], x[:, :half]], axis=1)
                o_ref[:, c0 + j:c0 + j + T_DK] = x * cos + swapped * sin
        elif c0 < n_f32:
            o_ref[:, c0:c0 + cw] = z * _sigmoid(z)
        else:
            ob_ref[:, c0 - n_f32:c0 - n_f32 + cw] = z.astype(BF16)


def _norm_proj(h, g, w, wgt=None, rotary=None):
    m, d = h.shape
    n = w.shape[1]
    tm = ROW_TILE
    in_specs = [pl.BlockSpec((tm, d), lambda i: (i, 0)), _const_spec((1, d)), _const_spec((d, n))]
    params = pltpu.CompilerParams(dimension_semantics=("arbitrary",), vmem_limit_bytes=VMEM_LIMIT)
    if wgt is None:
        n_f32 = n - T_WV
        table = pl.BlockSpec((tm, T_DK), lambda i: (i, 0))
        return pl.pallas_call(
            _norm_proj_retention_kernel, grid=(m // tm,), in_specs=in_specs + [table, table],
            out_specs=[pl.BlockSpec((tm, n_f32), lambda i: (i, 0)),
                       pl.BlockSpec((tm, T_WV), lambda i: (i, 0))],
            out_shape=[jax.ShapeDtypeStruct((m, n_f32), F32),
                       jax.ShapeDtypeStruct((m, T_WV), BF16)],
            scratch_shapes=[pltpu.VMEM((tm, d), BF16)],
            compiler_params=params, name="norm_proj")(h, g, w, *rotary)
    return pl.pallas_call(
        _norm_proj_gates_kernel, grid=(m // tm,),
        in_specs=in_specs + [_const_spec(wgt.shape)],
        out_specs=[pl.BlockSpec((tm, n), lambda i: (i, 0)),
                   pl.BlockSpec((SUBLANES, tm), lambda i: (0, i))],
        out_shape=[jax.ShapeDtypeStruct((m, n), F32), jax.ShapeDtypeStruct((SUBLANES, m), F32)],
        scratch_shapes=[pltpu.VMEM((tm, d), BF16)],
        compiler_params=params, name="norm_proj_gates")(h, g, w, wgt)


def _gate_act(z, is_input_gate):
    capped = GATE_CAP * jnp.tanh(z / GATE_CAP)
    return jnp.where(is_input_gate, capped, -_softplus(-capped))


def _mlstm_kernel(q_ref, k_ref, v_ref, og_ref, gc_ref, gr_ref, bc_ref, br_ref, mn_ref,
                  out_ref, c_ref, m_ref, h_sv, h_kv, h_q, h_maxd, h_bcum, h_row, h_gate):
    t = pl.program_id(1)
    tt = q_ref.shape[0]

    @pl.when(t == 0)
    def _():
        c_ref[...] = jnp.zeros_like(c_ref)
        m_ref[...] = jnp.zeros_like(m_ref)

    cs = M_CHUNK
    ri = lax.broadcasted_iota(jnp.int32, (cs, cs), 0)
    ci = lax.broadcasted_iota(jnp.int32, (cs, cs), 1)
    causal = ci <= ri
    lower = jnp.where(causal, 1.0, 0.0).astype(BF16)
    upper = jnp.where(ri <= ci, 1.0, 0.0).astype(BF16)
    n_blocks = 2 * M_HEADS
    sel_r = lax.broadcasted_iota(jnp.int32, (LANES, n_blocks * LANES), 0)
    sel_c = lax.broadcasted_iota(jnp.int32, (LANES, n_blocks * LANES), 1)
    spread = jnp.where(sel_r == sel_c // LANES, 1.0, 0.0).astype(BF16)
    ones_blk = jnp.ones((cs, LANES), BF16)
    col_lane = lax.broadcasted_iota(jnp.int32, (1, LANES), 1)
    col_row = lax.broadcasted_iota(jnp.int32, (cs, 1), 0)
    row_sub = lax.broadcasted_iota(jnp.int32, (SUBLANES, 1), 0)
    row_lane = lax.broadcasted_iota(jnp.int32, (1, cs), 1)
    scale = M_DH ** -0.5
    heads = range(M_HEADS)
    hs = [slice(h * M_DH, (h + 1) * M_DH) for h in heads]
    blk = lambda x, j: x[:, j * LANES:(j + 1) * LANES]
    row0 = lambda c: c * cs if isinstance(c, int) else pl.multiple_of(c * cs, cs)
    finite = lambda x: jnp.where(x == -jnp.inf, 0.0, x)

    def stage_a(c):
        r0 = row0(c)
        pos0 = t * tt + r0
        gact = _gate_act(gc_ref[pl.ds(r0, cs), :] + bc_ref[...], col_lane < M_HEADS)
        spread_g = _dot_select(gact, spread)
        valid = pos0 + col_row >= PAD
        li_b = [jnp.where(valid, blk(spread_g, h), -jnp.inf) for h in heads]
        lf_b = jnp.where(valid, spread_g[:, M_HEADS * LANES:], 0.0)
        bcum_all = _select_dot(lower, lf_b)
        bcum_b = [blk(bcum_all, h) for h in heads]
        gr = _gate_act(gr_ref[c] + br_ref[...], row_sub < M_HEADS)
        gr = jnp.where(pos0 + row_lane >= PAD, gr, jnp.where(row_sub < M_HEADS, -jnp.inf, 0.0))
        bcum_r = _dot_select(jnp.where(row_sub < M_HEADS, 0.0, gr), upper)
        yield
        qb = [q_ref[pl.ds(r0, cs), hs[h]].astype(BF16) for h in heads]
        k = [k_ref[pl.ds(r0, cs), hs[h]] * scale for h in heads]
        vb = [jnp.concatenate([v_ref[pl.ds(r0, cs), hs[h]].astype(BF16), ones_blk], axis=1)
              for h in heads]
        qk = [_bdot_nt(qb[h], k[h]) for h in heads]
        yield
        brow = [bcum_r[M_HEADS + h:M_HEADS + h + 1, :] for h in heads]
        lirow = [gr[h:h + 1, :] for h in heads]
        g_b = [bcum_b[h][cs - 1:cs, :] for h in heads]
        dmat = [jnp.where(causal, bcum_b[h][:, :cs] - brow[h] + lirow[h], -jnp.inf)
                for h in heads]
        maxd = [jnp.max(dmat[h], axis=-1, keepdims=True) for h in heads]
        a_row = [g_b[h][:, :1] - brow[h] + lirow[h] for h in heads]
        amax = [jnp.max(a_row[h], axis=-1, keepdims=True) for h in heads]
        yield
        sp = [qk[h] * jnp.exp(dmat[h] - finite(maxd[h])) for h in heads]
        kwp = [k[h] * jnp.exp(g_b[h] - bcum_b[h] + li_b[h] - finite(amax[h])) for h in heads]
        svp = [_bdot(sp[h], vb[h]) for h in heads]
        yield
        kvp = [_bdot_tn(kwp[h], vb[h]) for h in heads]
        yield
        for h in heads:
            og = og_ref[pl.ds(r0, cs), hs[h]]
            h_sv[h] = svp[h]
            h_kv[h] = kvp[h]
            h_q[h] = qb[h]
            h_maxd[h] = jnp.broadcast_to(maxd[h], (cs, LANES))
            h_bcum[h] = bcum_b[h]
            h_row[h] = jnp.concatenate([jnp.broadcast_to(amax[h], (1, LANES)), g_b[h]], axis=0)
            h_gate[h] = mn_ref[:, hs[h]] * _sigmoid(og)

    def stage_b(c):
        r0 = row0(c)
        svp = [h_sv[h] for h in heads]
        kvp = [h_kv[h] for h in heads]
        qb = [h_q[h] for h in heads]
        maxd = [h_maxd[h] for h in heads]
        bcum_b = [h_bcum[h] for h in heads]
        amax = [h_row[h][0:1, :] for h in heads]
        g_b = [h_row[h][1:2, :] for h in heads]
        gate = [h_gate[h] for h in heads]
        c_st = [c_ref[h] for h in heads]
        m_st = [m_ref[h] for h in heads]
        qcn = [_bdot(qb[h], c_st[h]) for h in heads]
        inter = [bcum_b[h] + m_st[h] for h in heads]
        m_row = [jnp.maximum(inter[h], maxd[h]) for h in heads]
        yield
        m_new = [jnp.maximum(g_b[h] + m_st[h], amax[h]) for h in heads]
        for h in heads:
            decay = jnp.exp(g_b[h] + m_st[h] - m_new[h])[:, :1]
            grow = jnp.exp(finite(amax[h]) - m_new[h])[:, :1]
            c_ref[h] = decay * c_st[h] + grow * kvp[h]
            m_ref[h] = m_new[h]
        yield
        f_intra = [jnp.exp(finite(maxd[h]) - m_row[h]) for h in heads]
        w_inter = [jnp.exp(inter[h] - m_row[h]) for h in heads]
        yield
        for h in heads:
            num = f_intra[h] * svp[h][:, :M_DH] + w_inter[h] * qcn[h][:, :M_DH]
            den = f_intra[h] * svp[h][:, M_DH:] + w_inter[h] * qcn[h][:, M_DH:]
            hc = num / jnp.maximum(jnp.abs(den), jnp.exp(-m_row[h]))
            hn = hc * lax.rsqrt(jnp.mean(hc * hc, axis=-1, keepdims=True) + NORM_EPS)
            out_ref[pl.ds(r0, cs), hs[h]] = (hn * gate[h]).astype(BF16)

    def steady(c, carry):
        _run_stages(stage_b(c), stage_a(c + 1))
        return carry

    n_chunks = tt // cs
    _run_stages(stage_a(0))
    lax.fori_loop(0, n_chunks - 1, steady, 0)
    _run_stages(stage_b(n_chunks - 1))


def _mlstm(z, gt_chunks, bias_col, bias_row, m_norm, batch, lp):
    m = z.shape[0]
    tt = LONG_TILE if lp % LONG_TILE == 0 else SEQ_TILE
    nt = lp // tt
    row = lambda b, t: b * nt + t
    gate_col_block = 4 * M_W // LANES + (3 * R_W + 256) // LANES
    handoff = [pltpu.VMEM((M_HEADS, M_CHUNK, 2 * M_DH), F32),
               pltpu.VMEM((M_HEADS, M_DH, 2 * M_DH), F32),
               pltpu.VMEM((M_HEADS, M_CHUNK, M_DH), BF16),
               pltpu.VMEM((M_HEADS, M_CHUNK, LANES), F32),
               pltpu.VMEM((M_HEADS, M_CHUNK, LANES), F32),
               pltpu.VMEM((M_HEADS, 2, LANES), F32),
               pltpu.VMEM((M_HEADS, M_CHUNK, M_DH), F32)]
    in_specs = [
        pl.BlockSpec((tt, M_W), lambda b, t: (row(b, t), 0)),
        pl.BlockSpec((tt, M_W), lambda b, t: (row(b, t), 1)),
        pl.BlockSpec((tt, M_W), lambda b, t: (row(b, t), 2)),
        pl.BlockSpec((tt, M_W), lambda b, t: (row(b, t), 3)),
        pl.BlockSpec((tt, LANES), lambda b, t: (row(b, t), gate_col_block)),
        pl.BlockSpec((tt // M_CHUNK, SUBLANES, M_CHUNK), lambda b, t: (row(b, t), 0, 0)),
        _const_spec((1, LANES)), _const_spec((SUBLANES, M_CHUNK)), _const_spec((1, M_W)),
    ]
    return pl.pallas_call(
        _mlstm_kernel, grid=(batch, nt), in_specs=in_specs,
        out_specs=pl.BlockSpec((tt, M_W), lambda b, t: (row(b, t), 0)),
        out_shape=jax.ShapeDtypeStruct((m, M_W), BF16),
        scratch_shapes=[pltpu.VMEM((M_HEADS, M_DH, 2 * M_DH), F32),
                        pltpu.VMEM((M_HEADS, 1, LANES), F32)] + handoff,
        compiler_params=pltpu.CompilerParams(dimension_semantics=("arbitrary", "arbitrary"),
                                             vmem_limit_bytes=VMEM_LIMIT),
        name="mlstm")(z, z, z, z, z, gt_chunks, bias_col, bias_row, m_norm)


def _rwkv_kernel(r_ref, k_ref, v_ref, wa_ref, g_ref, mur_ref, muk_ref, muv_ref, muwa_ref, mug_ref,
                 w0_ref, w2_ref, a0_ref, a2_ref, g2_ref, kk_ref, ka_ref, rk_ref, lnw_ref, lnb_ref,
                 out_ref, cr_ref, ck_ref, cv_ref, cwa_ref, cg_ref, st_ref,
                 h_left, h_vst, h_end, h_wrb, h_onv, h_apow, h_tinv, h_pend, h_bonus, h_gg):
    t = pl.program_id(1)
    tt = r_ref.shape[0]
    two_c = 2 * CHUNK

    @pl.when(t == 0)
    def _():
        for ref in (cr_ref, ck_ref, cv_ref, cwa_ref, cg_ref, st_ref):
            ref[...] = jnp.zeros_like(ref)

    row_i = lax.broadcasted_iota(jnp.int32, (CHUNK, 1), 0)
    row_8 = lax.broadcasted_iota(jnp.int32, (SUBLANES, 1), 0)
    lo = lax.broadcasted_iota(jnp.int32, (1, LANES), 1) < R_DH
    ri = lax.broadcasted_iota(jnp.int32, (CHUNK, CHUNK), 0)
    ci = lax.broadcasted_iota(jnp.int32, (CHUNK, CHUNK), 1)
    lower = (ci <= ri).astype(BF16)
    r4 = lax.broadcasted_iota(jnp.int32, (4 * CHUNK, 4 * CHUNK), 0)
    c4 = lax.broadcasted_iota(jnp.int32, (4 * CHUNK, 4 * CHUNK), 1)
    same_head = ((r4 // CHUNK) % 2) == ((c4 // CHUNK) % 2)
    below = (c4 % CHUNK < r4 % CHUNK) | ((r4 >= two_c) & (c4 % CHUNK == r4 % CHUNK))
    keep = jnp.where(same_head & below, 1.0, 0.0).astype(BF16)
    r2 = lax.broadcasted_iota(jnp.int32, (two_c, two_c), 0)
    c2 = lax.broadcasted_iota(jnp.int32, (two_c, two_c), 1)
    eye2 = (r2 == c2).astype(F32)
    decay_scale = float(np.exp(-0.5))

    def head_sum(x):
        parts = []
        for p in range(R_PAIRS):
            xp = x[:, p * LANES:(p + 1) * LANES]
            s0 = jnp.sum(jnp.where(lo, xp, 0.0), axis=-1, keepdims=True)
            s1 = jnp.sum(jnp.where(lo, 0.0, xp), axis=-1, keepdims=True)
            parts.append(jnp.where(lo, s0, s1))
        return jnp.concatenate(parts, axis=1)

    def stack_heads(x):
        return jnp.concatenate([jnp.where(lo, x, 0.0), jnp.where(lo, 0.0, x)], axis=0)

    pairs = range(R_PAIRS)
    pair = lambda x, p: x[:, p * LANES:(p + 1) * LANES]
    bf = lambda x: x.astype(BF16)
    row0 = lambda c: c * CHUNK if isinstance(c, int) else pl.multiple_of(c * CHUNK, CHUNK)

    def double(apow, tinv):
        both = [_bdot(apow[p], jnp.concatenate([bf(apow[p]), bf(tinv[p])], axis=1)) for p in pairs]
        return [x[:, :two_c] for x in both], [tinv[p] + both[p][:, two_c:] for p in pairs]

    def stage_a(c):
        r0 = row0(c)

        def shift_mix(x_ref, c_ref, mu_ref):
            x = x_ref[pl.ds(r0, CHUNK), :]
            prev = pltpu.roll(x, 1, axis=0)
            head = jnp.where(row_8 == 0, c_ref[SUBLANES - 1:SUBLANES, :], prev[:SUBLANES])
            prev = jnp.concatenate([head, prev[SUBLANES:]], axis=0)
            c_ref[...] = x[CHUNK - SUBLANES:CHUNK, :]
            return x + (prev - x) * mu_ref[...]

        xwa = shift_mix(wa_ref, cwa_ref, muwa_ref)
        y_w = _bdot(jnp.tanh(xwa), w2_ref[...])
        y_a = _bdot(xwa, a2_ref[...])
        yield
        xg = shift_mix(g_ref, cg_ref, mug_ref)
        gg = _bdot(_sigmoid(xg), g2_ref[...])
        yield
        rr = shift_mix(r_ref, cr_ref, mur_ref)
        yield
        kr = shift_mix(k_ref, ck_ref, muk_ref)
        kk = kr * kk_ref[...]
        yield
        kk = kk * lax.rsqrt(jnp.maximum(head_sum(kk * kk), 1e-24))
        yield
        vr = shift_mix(v_ref, cv_ref, muv_ref)
        yield
        lw = -decay_scale * _sigmoid(w0_ref[...] + y_w)
        cum = _select_dot(lower, lw)
        yield
        aa = _sigmoid(a0_ref[...] + y_a)
        k2 = kr * (1.0 + (aa - 1.0) * ka_ref[...])
        yield
        bonus = head_sum(rr * k2 * rk_ref[...]) * vr
        yield
        e_pos = jnp.exp(cum)
        e_neg = jnp.exp(-cum)
        p_end = e_pos[CHUNK - 1:CHUNK, :]
        yield
        alpha = -kk * jnp.exp(cum - lw)
        rb = rr * e_pos
        yield
        beta = kk * aa * e_neg
        kt = k2 * e_neg
        yield
        beta_end = beta * p_end
        kt_end = kt * p_end
        yield
        left, prod = [], []
        for p in pairs:
            left.append(jnp.concatenate([stack_heads(pair(alpha, p)), stack_heads(pair(rb, p))],
                                        axis=0).astype(BF16))
            right = jnp.concatenate([pair(beta, p)] * 2 + [pair(kt, p)] * 2, axis=0)
            prod.append(_bdot_nt(left[p], right))
            yield
        v_st, end_st = [], []
        for p in pairs:
            v_st.append(stack_heads(pair(vr, p)).astype(BF16))
            end_st.append(jnp.concatenate([stack_heads(pair(beta_end, p)),
                                           stack_heads(pair(kt_end, p))], axis=0).astype(BF16))
            yield
        w_rb, tinv, apow, on_v = [], [], [], []
        for p in pairs:
            masked = prod[p].astype(BF16) * keep
            a_ab = masked[:two_c, :two_c]
            w_rb.append(masked[two_c:, :two_c])
            tinv.append(eye2 + a_ab.astype(F32))
            apow.append(_bdot(a_ab, a_ab))
            on_v.append(_bdot(masked[:, two_c:], v_st[p]))
            yield
        for _ in range(DOUBLINGS_AHEAD):
            apow, tinv = double(apow, tinv)
            yield
        for p in pairs:
            h_left[p] = left[p]
            h_vst[p] = v_st[p]
            h_end[p] = end_st[p]
            h_wrb[p] = w_rb[p]
            h_onv[p] = on_v[p]
            h_apow[p] = bf(apow[p])
            h_tinv[p] = tinv[p]
        h_pend[...] = p_end
        h_bonus[...] = bonus
        h_gg[...] = gg

    def stage_b(c):
        r0 = row0(c)
        left = [h_left[p] for p in pairs]
        v_st = [h_vst[p] for p in pairs]
        end_st = [h_end[p] for p in pairs]
        w_rb = [h_wrb[p] for p in pairs]
        on_v = [h_onv[p] for p in pairs]
        apow = [h_apow[p] for p in pairs]
        tinv = [h_tinv[p] for p in pairs]
        p_end = h_pend[...]
        bonus = h_bonus[...]
        gg = h_gg[...]
        st = [st_ref[p] for p in pairs]
        for _ in range(4 - DOUBLINGS_AHEAD):
            apow, tinv = double(apow, tinv)
            yield
        on_s = [_bdot_nt(left[p], st[p]) for p in pairs]
        tinv = [tinv[p] + _bdot(apow[p], tinv[p]) for p in pairs]
        yield
        u = [_bdot(tinv[p], on_s[p][:two_c] + on_v[p][:two_c]) for p in pairs]
        yield
        o_st = [on_s[p][two_c:] + on_v[p][two_c:] + _bdot(w_rb[p], u[p]) for p in pairs]
        for p in pairs:
            uv = jnp.concatenate([bf(u[p]), v_st[p]], axis=0)
            st_ref[p] = pair(p_end, p) * st[p] + _bdot_tn(uv, end_st[p])
        yield
        o = jnp.concatenate([x[:CHUNK] + x[CHUNK:] for x in o_st], axis=1)
        mean = head_sum(o) * (1.0 / R_DH)
        oc = o - mean
        yield
        var = head_sum(oc * oc) * (1.0 / R_DH)
        y = oc * lax.rsqrt(var + R_LN_EPS) * lnw_ref[...] + lnb_ref[...] + bonus
        valid = (t * tt + r0 + row_i) >= PAD
        out_ref[pl.ds(r0, CHUNK), :] = jnp.where(valid, y * gg, 0.0).astype(BF16)

    def steady(c, carry):
        _run_stages(stage_b(c), stage_a(c + 1), steps=(1, A_STEPS_PER_B_STEP))
        return carry

    n_chunks = tt // CHUNK
    _run_stages(stage_a(0))
    lax.fori_loop(0, n_chunks - 1, steady, 0)
    _run_stages(stage_b(n_chunks - 1))


def _rwkv(z, vecs, mats, batch, lp):
    m = z.shape[0]
    tt = LONG_TILE if lp % LONG_TILE == 0 else SEQ_TILE
    nt = lp // tt
    row = lambda b, t: b * nt + t
    base = 4 * M_W
    two_c = 2 * CHUNK
    handoff = [pltpu.VMEM((R_PAIRS, 2 * two_c, LANES), BF16),
               pltpu.VMEM((R_PAIRS, two_c, LANES), BF16),
               pltpu.VMEM((R_PAIRS, 2 * two_c, LANES), BF16),
               pltpu.VMEM((R_PAIRS, two_c, two_c), BF16),
               pltpu.VMEM((R_PAIRS, 2 * two_c, LANES), F32),
               pltpu.VMEM((R_PAIRS, two_c, two_c), BF16),
               pltpu.VMEM((R_PAIRS, two_c, two_c), F32),
               pltpu.VMEM((1, R_W), F32),
               pltpu.VMEM((CHUNK, R_W), F32),
               pltpu.VMEM((CHUNK, R_W), F32)]
    in_specs = [
        pl.BlockSpec((tt, R_W), lambda b, t: (row(b, t), base // R_W)),
        pl.BlockSpec((tt, R_W), lambda b, t: (row(b, t), base // R_W + 1)),
        pl.BlockSpec((tt, R_W), lambda b, t: (row(b, t), base // R_W + 2)),
        pl.BlockSpec((tt, LANES), lambda b, t: (row(b, t), (base + 3 * R_W) // LANES)),
        pl.BlockSpec((tt, LANES), lambda b, t: (row(b, t), (base + 3 * R_W) // LANES + 1)),
    ]
    mu_r, mu_k, mu_v, mu_wa, mu_g, w0, a0, k_k, k_a, r_k, ln_w, ln_b = vecs
    w2, a2, g2 = mats
    args = [mu_r, mu_k, mu_v, mu_wa, mu_g, w0, w2, a0, a2, g2, k_k, k_a, r_k, ln_w, ln_b]
    in_specs += [_const_spec(a.shape) for a in args]
    return pl.pallas_call(
        _rwkv_kernel, grid=(batch, nt), in_specs=in_specs,
        out_specs=pl.BlockSpec((tt, R_W), lambda b, t: (row(b, t), 0)),
        out_shape=jax.ShapeDtypeStruct((m, R_W), BF16),
        scratch_shapes=[pltpu.VMEM((SUBLANES, R_W), F32), pltpu.VMEM((SUBLANES, R_W), F32),
                        pltpu.VMEM((SUBLANES, R_W), F32), pltpu.VMEM((SUBLANES, LANES), F32),
                        pltpu.VMEM((SUBLANES, LANES), F32),
                        pltpu.VMEM((R_PAIRS, LANES, LANES), F32)] + handoff,
        compiler_params=pltpu.CompilerParams(dimension_semantics=("arbitrary", "arbitrary"),
                                             vmem_limit_bytes=VMEM_LIMIT),
        name="rwkv7")(z, z, z, z, z, *args)


def _retention_kernel(q_ref, k_ref, gate_ref, v_ref, out_ref, st_ref):
    t = pl.program_id(1)
    c = q_ref.shape[0]

    @pl.when(t == 0)
    def _():
        st_ref[...] = jnp.zeros_like(st_ref)

    ri = lax.broadcasted_iota(jnp.int32, (c, c), 0)
    ci = lax.broadcasted_iota(jnp.int32, (c, c), 1)
    causal = jnp.where(ci <= ri, 1.0, 0.0).astype(BF16)
    steps = lax.broadcasted_iota(jnp.int32, (c, 1), 0).astype(F32) + 1.0

    for h in range(T_HEADS):
        log_gamma = float(np.log(1.0 - 2.0 ** (-5.0 - h)))
        ks = slice(h * T_DK, (h + 1) * T_DK)
        vs = slice(h * T_DV, (h + 1) * T_DV)
        q = (q_ref[:, ks] * jnp.exp(log_gamma * steps)).astype(BF16)
        k = (k_ref[:, ks] * (jnp.exp(-log_gamma * steps) * (T_DK ** -0.5))).astype(BF16)
        vb = v_ref[:, vs]
        st = st_ref[h]
        s = _bdot_nt(q, k).astype(BF16) * causal
        o = _bdot(s, vb) + _bdot(q, st)
        st_ref[h] = float(np.exp(log_gamma * c)) * (st + _bdot_tn(k, vb))

        mu = jnp.mean(o, axis=-1, keepdims=True)
        oc = o - mu
        var = jnp.mean(oc * oc, axis=-1, keepdims=True)
        out_ref[:, vs] = (oc * lax.rsqrt(var + NORM_EPS) * gate_ref[:, vs]).astype(BF16)


def _retention(z, v, batch, lp):
    m = z.shape[0]
    tt = SEQ_TILE
    nt = lp // tt
    row = lambda b, t: b * nt + t
    in_specs = [
        pl.BlockSpec((tt, D_MODEL), lambda b, t: (row(b, t), 0)),
        pl.BlockSpec((tt, D_MODEL), lambda b, t: (row(b, t), 1)),
        pl.BlockSpec((tt, T_WV), lambda b, t: (row(b, t), 1)),
        pl.BlockSpec((tt, T_WV), lambda b, t: (row(b, t), 0)),
    ]
    return pl.pallas_call(
        _retention_kernel, grid=(batch, nt), in_specs=in_specs,
        out_specs=pl.BlockSpec((tt, T_WV), lambda b, t: (row(b, t), 0)),
        out_shape=jax.ShapeDtypeStruct((m, T_WV), BF16),
        scratch_shapes=[pltpu.VMEM((T_HEADS, T_DK, T_DV), F32)],
        compiler_params=pltpu.CompilerParams(dimension_semantics=("arbitrary", "arbitrary"),
                                             vmem_limit_bytes=VMEM_LIMIT),
        name="retention")(z, z, z, v)


def _ffn_kernel(n_mix, *refs):
    h_ref = refs[0]
    a_refs = refs[1:1 + n_mix]
    wo_refs = refs[1 + n_mix:1 + 2 * n_mix]
    g_ref, wv_ref, wg_ref, cw_ref, cb_ref, wd_ref, o_ref, carry_ref, hn_ref, act_ref = (
        refs[1 + 2 * n_mix:])
    tm = h_ref.shape[0]

    @pl.when(pl.program_id(0) == 0)
    def _():
        carry_ref[...] = jnp.zeros_like(carry_ref)

    mixed = h_ref[...]
    for a_ref, wo_ref in zip(a_refs, wo_refs):
        mixed = mixed + jnp.dot(a_ref[...], wo_ref[...], preferred_element_type=F32)
    o_ref[...] = mixed
    hn_ref[...] = _rmsnorm_bf16(mixed, g_ref[...])
    row = lax.broadcasted_iota(jnp.int32, (SUBLANES, 1), 0)

    for c in range(N_FF_CHUNKS):
        cs = slice(c * FF_CHUNK, (c + 1) * FF_CHUNK)
        hn = hn_ref[...]
        val = jnp.dot(hn, wv_ref[:, cs], preferred_element_type=F32)
        gate = jnp.dot(hn, wg_ref[:, cs], preferred_element_type=F32)
        tail = carry_ref[:, cs]
        prev1 = pltpu.roll(gate, 1, axis=0)
        prev2 = pltpu.roll(gate, 2, axis=0)
        head1 = jnp.where(row == 0, tail[SUBLANES - 1:SUBLANES, :], prev1[:SUBLANES])
        head2 = jnp.where(row == 0, tail[SUBLANES - 2:SUBLANES - 1, :], prev2[:SUBLANES])
        head2 = jnp.where(row == 1, tail[SUBLANES - 1:SUBLANES, :], head2)
        prev1 = jnp.concatenate([head1, prev1[SUBLANES:]], axis=0)
        prev2 = jnp.concatenate([head2, prev2[SUBLANES:]], axis=0)
        carry_ref[:, cs] = gate[tm - SUBLANES:tm, :]
        conv = (cw_ref[2:3, cs] * gate + cw_ref[1:2, cs] * prev1 + cw_ref[0:1, cs] * prev2
                + cb_ref[:, cs])
        act_ref[:, cs] = (conv * _sigmoid(conv) * val).astype(BF16)
    o_ref[...] += jnp.dot(act_ref[...], wd_ref[...], preferred_element_type=F32)


def _mix_ffn(h, acts, w_outs, g, wv, wg, cw, cb, wd):
    m, d = h.shape
    tm = ROW_TILE
    in_specs = [pl.BlockSpec((tm, d), lambda i: (i, 0))]
    in_specs += [pl.BlockSpec((tm, a.shape[1]), lambda i: (i, 0)) for a in acts]
    in_specs += [_const_spec(a.shape) for a in (*w_outs, g, wv, wg, cw, cb, wd)]
    return pl.pallas_call(
        functools.partial(_ffn_kernel, len(acts)), grid=(m // tm,), in_specs=in_specs,
        out_specs=pl.BlockSpec((tm, d), lambda i: (i, 0)),
        out_shape=jax.ShapeDtypeStruct((m, d), F32),
        scratch_shapes=[pltpu.VMEM((SUBLANES, D_FF), F32),
                        pltpu.VMEM((tm, d), BF16),
                        pltpu.VMEM((tm, D_FF), BF16)],
        compiler_params=pltpu.CompilerParams(dimension_semantics=("arbitrary",),
                                             vmem_limit_bytes=VMEM_LIMIT),
        name="mix_ffn")(h, *acts, *w_outs, g, wv, wg, cw, cb, wd)


def _final_norm_kernel(h_ref, g_ref, o_ref):
    x = h_ref[...]
    ms = jnp.mean(x * x, axis=-1, keepdims=True)
    o_ref[0] = x * lax.rsqrt(ms + NORM_EPS) * g_ref[...]


def _final_norm(h, g, batch, lp):
    d = h.shape[1]
    seq = lp - FRONT
    rows = 2 * ROW_TILE if seq % (2 * ROW_TILE) == 0 else FRONT
    return pl.pallas_call(
        _final_norm_kernel, grid=(batch, seq // rows),
        in_specs=[pl.BlockSpec((pl.Element(rows), pl.Element(d)),
                               lambda b, j: (pl.multiple_of(b * lp + FRONT + j * rows, FRONT), 0)),
                  _const_spec((1, d))],
        out_specs=pl.BlockSpec((1, rows, d), lambda b, j: (b, j, 0)),
        out_shape=jax.ShapeDtypeStruct((batch, seq, d), F32),
        compiler_params=pltpu.CompilerParams(dimension_semantics=("arbitrary", "arbitrary")),
        name="final_norm")(h, g)


def _ffn_weights(w_up, conv_w, conv_b, w_down):
    wv = w_up[:, :D_FF].astype(BF16)
    wg = w_up[:, D_FF:].astype(BF16)
    cw = jnp.pad(conv_w, ((0, SUBLANES - conv_w.shape[0]), (0, 0)))
    return wv, wg, cw, conv_b.reshape(1, D_FF), w_down.astype(BF16)


def kernel(x, meta_tokens, norm_mix, norm_ffn, norm_final, e_w_in, e_w_out, m_b_i, m_b_f, m_norm,
           r_mu, r_w0, r_w2, r_a0, r_a2, r_g2, r_k_k, r_k_a, r_r_k, r_ln_w, r_ln_b, o_w_in, o_w_out,
           f_w_up, f_conv_w, f_conv_b, f_w_down):
    batch, seq, d = x.shape
    lp = seq + FRONT
    assert d == D_MODEL and lp % SEQ_TILE == 0 and (batch * lp) % ROW_TILE == 0
    m = batch * lp
    row = lambda v: v.reshape(1, -1).astype(F32)

    meta = jnp.broadcast_to(meta_tokens[None].astype(x.dtype), (batch, N_META, d))
    h = jnp.concatenate([jnp.zeros((batch, PAD, d), x.dtype), meta, x], axis=1).reshape(m, d)

    w_in = e_w_in[0]
    n_m = 4 * M_W
    gates_w = w_in[:, n_m:n_m + 2 * M_HEADS]
    w0 = jnp.concatenate([w_in[:, :n_m], w_in[:, n_m + 2 * M_HEADS:], gates_w,
                          jnp.zeros((d, LANES - 2 * M_HEADS), F32)], axis=1).astype(BF16)
    z, gt = _norm_proj(h, row(norm_mix[0]), w0, gates_w.T.astype(BF16))
    gt_chunks = gt.reshape(SUBLANES, m // M_CHUNK, M_CHUNK).transpose(1, 0, 2)
    gate_bias = jnp.concatenate([m_b_i[0], m_b_f[0]])
    bias_col = jnp.pad(gate_bias, (0, LANES - 2 * M_HEADS)).reshape(1, LANES)
    bias_row = jnp.broadcast_to(gate_bias[:, None], (SUBLANES, M_CHUNK))
    mix_m = _mlstm(z, gt_chunks, bias_col, bias_row, row(m_norm[0]), batch, lp)

    mu = r_mu[0]
    rank_wa = r_w2.shape[1] + r_a2.shape[1]
    vecs = [row(mu[:R_W]), row(mu[R_W:2 * R_W]), row(mu[2 * R_W:3 * R_W]),
            row(mu[3 * R_W:3 * R_W + rank_wa]), row(mu[3 * R_W + rank_wa:]),
            row(r_w0[0]), row(r_a0[0]), row(r_k_k[0]), row(r_k_a[0]), row(r_r_k[0]),
            row(r_ln_w[0]), row(r_ln_b[0])]
    w2 = jnp.concatenate([r_w2[0], jnp.zeros_like(r_a2[0])], axis=0).astype(BF16)
    a2 = jnp.concatenate([jnp.zeros_like(r_w2[0]), r_a2[0]], axis=0).astype(BF16)
    mix_r = _rwkv(z, vecs, [w2, a2, r_g2[0].astype(BF16)], batch, lp)

    w_out = e_w_out[0].astype(BF16)
    h = _mix_ffn(h, [mix_m, mix_r], [w_out[:M_W], w_out[M_W:]], row(norm_ffn[0]),
                 *_ffn_weights(f_w_up[0], f_conv_w[0], f_conv_b[0], f_w_down[0]))

    w_in = o_w_in[0]
    perm = np.concatenate([np.arange(0, T_DK, 2), np.arange(1, T_DK, 2)])
    qk_cols = np.concatenate([hh * T_DK + perm for hh in range(T_HEADS)])
    w1 = jnp.concatenate([w_in[:, qk_cols], w_in[:, D_MODEL + qk_cols],
                          w_in[:, 2 * D_MODEL + T_WV:], w_in[:, 2 * D_MODEL:2 * D_MODEL + T_WV]],
                         axis=1).astype(BF16)
    inv = 1.0 / (ROPE_BASE ** jnp.linspace(0.0, 1.0, T_DK // 2, dtype=F32))
    pos = jnp.arange(lp, dtype=F32) - PAD
    ang = pos[:, None] * inv[None, :]
    cos = jnp.tile(jnp.concatenate([jnp.cos(ang), jnp.cos(ang)], axis=1), (batch, 1))
    sin = jnp.tile(jnp.concatenate([-jnp.sin(ang), jnp.sin(ang)], axis=1), (batch, 1))
    z, v = _norm_proj(h, row(norm_mix[1]), w1, rotary=(cos, sin))
    o = _retention(z, v, batch, lp)
    h = _mix_ffn(h, [o], [o_w_out[0].astype(BF16)], row(norm_ffn[1]),
                 *_ffn_weights(f_w_up[1], f_conv_w[1], f_conv_b[1], f_w_down[1]))

    return _final_norm(h, row(norm_final), batch, lp)
```

```python
import functools

import numpy as np
import jax
import jax.numpy as jnp
from jax import lax
from jax.experimental import pallas as pl
from jax.experimental.pallas import tpu as pltpu

F32 = jnp.float32
BF16 = jnp.bfloat16

D_MODEL = 1024
N_META = 16
NORM_EPS = 1e-6
M_HEADS = 4
M_DH = 128
M_W = M_HEADS * M_DH
GATE_CAP = 15.0
R_DH = 64
R_HEADS = 8
R_W = R_HEADS * R_DH
R_PAIRS = R_HEADS // 2
R_LN_EPS = 64e-5
T_HEADS = 4
T_DK = 256
T_DV = 512
T_WV = T_HEADS * T_DV
ROPE_BASE = 10000.0
D_FF = 2816
FF_CHUNK = 256
N_FF_CHUNKS = D_FF // FF_CHUNK

LANES = 128
SUBLANES = 8
FRONT = 128
PAD = FRONT - N_META
CHUNK = 64
M_CHUNK = 128
ROW_TILE = 512
SEQ_TILE = 384
LONG_TILE = 1408
DOUBLINGS_AHEAD = 1
A_STEPS_PER_B_STEP = 4
VMEM_LIMIT = 56 * 1024 * 1024


def _const_spec(shape):
    nd = len(shape)
    return pl.BlockSpec(shape, lambda *_: (0,) * nd, pipeline_mode=pl.Buffered(1))


def _sigmoid(x):
    return 1.0 / (1.0 + jnp.exp(-x))


def _softplus(x):
    return jnp.maximum(x, 0.0) + jnp.log1p(jnp.exp(-jnp.abs(x)))


def _bdot(a, b):
    return jnp.dot(a.astype(BF16), b.astype(BF16), preferred_element_type=F32)


def _bdot_nt(a, b):
    return lax.dot_general(a.astype(BF16), b.astype(BF16), (((1,), (1,)), ((), ())),
                           preferred_element_type=F32)


def _bdot_tn(a, b):
    return lax.dot_general(a.astype(BF16), b.astype(BF16), (((0,), (0,)), ((), ())),
                           preferred_element_type=F32)


def _split3(x):
    hi = x.astype(BF16)
    rest = x - hi.astype(F32)
    mid = rest.astype(BF16)
    return hi, mid, (rest - mid.astype(F32)).astype(BF16)


def _select_dot(sel, x):
    hi, mid, low = _split3(x)
    dot = lambda y: jnp.dot(sel, y, preferred_element_type=F32)
    return dot(hi) + dot(mid) + dot(low)


def _dot_select(x, sel):
    hi, mid, low = _split3(x)
    dot = lambda y: jnp.dot(y, sel, preferred_element_type=F32)
    return dot(hi) + dot(mid) + dot(low)


def _run_stages(*stages, steps=None):
    steps = steps or (1,) * len(stages)
    live = list(zip(stages, steps))
    done = object()
    while live:
        live = [(s, n) for s, n in live if all(next(s, done) is not done for _ in range(n))]


def _rmsnorm_bf16(x, g):
    ms = jnp.mean(x * x, axis=-1, keepdims=True)
    return (x * lax.rsqrt(ms + NORM_EPS) * g).astype(BF16)


def _col_chunks(n, width=512):
    return [(c, min(width, n - c)) for c in range(0, n, width)]


def _norm_proj_kernel(h_ref, g_ref, w_ref, o_ref, hn_ref):
    hn_ref[...] = _rmsnorm_bf16(h_ref[...], g_ref[...])
    for c0, cw in _col_chunks(w_ref.shape[1]):
        o_ref[:, c0:c0 + cw] = jnp.dot(hn_ref[...], w_ref[:, c0:c0 + cw],
                                       preferred_element_type=F32)


def _norm_proj_gates_kernel(h_ref, g_ref, w_ref, wgt_ref, o_ref, gt_ref, hn_ref):
    _norm_proj_kernel(h_ref, g_ref, w_ref, o_ref, hn_ref)
    gt_ref[...] = lax.dot_general(wgt_ref[...], hn_ref[...], (((1,), (1,)), ((), ())),
                                  preferred_element_type=F32)


def _norm_proj_retention_kernel(h_ref, g_ref, w_ref, cos_ref, sin_ref, o_ref, ob_ref, hn_ref):
    hn_ref[...] = _rmsnorm_bf16(h_ref[...], g_ref[...])
    n_f32 = o_ref.shape[1]
    half = T_DK // 2
    for c0, cw in _col_chunks(w_ref.shape[1], 2 * T_DK):
        z = jnp.dot(hn_ref[...], w_ref[:, c0:c0 + cw], preferred_element_type=F32)
        if c0 < 2 * D_MODEL:
            cos = cos_ref[...]
            sin = sin_ref[...]
            for j in range(0, cw, T_DK):
                x = z[:, j:j + T_DK]
                swapped = jnp.concatenate([x[:, half:], x[:, :half]], axis=1)
                o_ref[:, c0 + j:c0 + j + T_DK] = x * cos + swapped * sin
        elif c0 < n_f32:
            o_ref[:, c0:c0 + cw] = z * _sigmoid(z)
        else:
            ob_ref[:, c0 - n_f32:c0 - n_f32 + cw] = z.astype(BF16)


def _norm_proj(h, g, w, wgt=None, rotary=None):
    m, d = h.shape
    n = w.shape[1]
    tm = ROW_TILE
    in_specs = [pl.BlockSpec((tm, d), lambda i: (i, 0)), _const_spec((1, d)), _const_spec((d, n))]
    params = pltpu.CompilerParams(dimension_semantics=("arbitrary",), vmem_limit_bytes=VMEM_LIMIT)
    if wgt is None:
        n_f32 = n - T_WV
        table = pl.BlockSpec((tm, T_DK), lambda i: (i, 0))
        return pl.pallas_call(
            _norm_proj_retention_kernel, grid=(m // tm,), in_specs=in_specs + [table, table],
            out_specs=[pl.BlockSpec((tm, n_f32), lambda i: (i, 0)),
                       pl.BlockSpec((tm, T_WV), lambda i: (i, 0))],
            out_shape=[jax.ShapeDtypeStruct((m, n_f32), F32),
                       jax.ShapeDtypeStruct((m, T_WV), BF16)],
            scratch_shapes=[pltpu.VMEM((tm, d), BF16)],
            compiler_params=params, name="norm_proj")(h, g, w, *rotary)
    return pl.pallas_call(
        _norm_proj_gates_kernel, grid=(m // tm,),
        in_specs=in_specs + [_const_spec(wgt.shape)],
        out_specs=[pl.BlockSpec((tm, n), lambda i: (i, 0)),
                   pl.BlockSpec((SUBLANES, tm), lambda i: (0, i))],
        out_shape=[jax.ShapeDtypeStruct((m, n), F32), jax.ShapeDtypeStruct((SUBLANES, m), F32)],
        scratch_shapes=[pltpu.VMEM((tm, d), BF16)],
        compiler_params=params, name="norm_proj_gates")(h, g, w, wgt)


def _gate_act(z, is_input_gate):
    capped = GATE_CAP * jnp.tanh(z / GATE_CAP)
    return jnp.where(is_input_gate, capped, -_softplus(-capped))


def _mlstm_kernel(q_ref, k_ref, v_ref, og_ref, gc_ref, gr_ref, bc_ref, br_ref, mn_ref,
                  out_ref, c_ref, m_ref, h_sv, h_kv, h_q, h_maxd, h_bcum, h_row, h_gate):
    t = pl.program_id(1)
    tt = q_ref.shape[0]

    @pl.when(t == 0)
    def _():
        c_ref[...] = jnp.zeros_like(c_ref)
        m_ref[...] = jnp.zeros_like(m_ref)

    cs = M_CHUNK
    ri = lax.broadcasted_iota(jnp.int32, (cs, cs), 0)
    ci = lax.broadcasted_iota(jnp.int32, (cs, cs), 1)
    causal = ci <= ri
    lower = jnp.where(causal, 1.0, 0.0).astype(BF16)
    upper = jnp.where(ri <= ci, 1.0, 0.0).astype(BF16)
    n_blocks = 2 * M_HEADS
    sel_r = lax.broadcasted_iota(jnp.int32, (LANES, n_blocks * LANES), 0)
    sel_c = lax.broadcasted_iota(jnp.int32, (LANES, n_blocks * LANES), 1)
    spread = jnp.where(sel_r == sel_c // LANES, 1.0, 0.0).astype(BF16)
    ones_blk = jnp.ones((cs, LANES), BF16)
    col_lane = lax.broadcasted_iota(jnp.int32, (1, LANES), 1)
    col_row = lax.broadcasted_iota(jnp.int32, (cs, 1), 0)
    row_sub = lax.broadcasted_iota(jnp.int32, (SUBLANES, 1), 0)
    row_lane = lax.broadcasted_iota(jnp.int32, (1, cs), 1)
    scale = M_DH ** -0.5
    heads = range(M_HEADS)
    hs = [slice(h * M_DH, (h + 1) * M_DH) for h in heads]
    blk = lambda x, j: x[:, j * LANES:(j + 1) * LANES]
    row0 = lambda c: c * cs if isinstance(c, int) else pl.multiple_of(c * cs, cs)
    finite = lambda x: jnp.where(x == -jnp.inf, 0.0, x)

    def stage_a(c):
        r0 = row0(c)
        pos0 = t * tt + r0
        gact = _gate_act(gc_ref[pl.ds(r0, cs), :] + bc_ref[...], col_lane < M_HEADS)
        spread_g = _dot_select(gact, spread)
        valid = pos0 + col_row >= PAD
        li_b = [jnp.where(valid, blk(spread_g, h), -jnp.inf) for h in heads]
        lf_b = jnp.where(valid, spread_g[:, M_HEADS * LANES:], 0.0)
        bcum_all = _select_dot(lower, lf_b)
        bcum_b = [blk(bcum_all, h) for h in heads]
        gr = _gate_act(gr_ref[c] + br_ref[...], row_sub < M_HEADS)
        gr = jnp.where(pos0 + row_lane >= PAD, gr, jnp.where(row_sub < M_HEADS, -jnp.inf, 0.0))
        bcum_r = _dot_select(jnp.where(row_sub < M_HEADS, 0.0, gr), upper)
        yield
        qb = [q_ref[pl.ds(r0, cs), hs[h]].astype(BF16) for h in heads]
        k = [k_ref[pl.ds(r0, cs), hs[h]] * scale for h in heads]
        vb = [jnp.concatenate([v_ref[pl.ds(r0, cs), hs[h]].astype(BF16), ones_blk], axis=1)
              for h in heads]
        qk = [_bdot_nt(qb[h], k[h]) for h in heads]
        yield
        brow = [bcum_r[M_HEADS + h:M_HEADS + h + 1, :] for h in heads]
        lirow = [gr[h:h + 1, :] for h in heads]
        g_b = [bcum_b[h][cs - 1:cs, :] for h in heads]
        dmat = [jnp.where(causal, bcum_b[h][:, :cs] - brow[h] + lirow[h], -jnp.inf)
                for h in heads]
        maxd = [jnp.max(dmat[h], axis=-1, keepdims=True) for h in heads]
        a_row = [g_b[h][:, :1] - brow[h] + lirow[h] for h in heads]
        amax = [jnp.max(a_row[h], axis=-1, keepdims=True) for h in heads]
        yield
        sp = [qk[h] * jnp.exp(dmat[h] - finite(maxd[h])) for h in heads]
        kwp = [k[h] * jnp.exp(g_b[h] - bcum_b[h] + li_b[h] - finite(amax[h])) for h in heads]
        svp = [_bdot(sp[h], vb[h]) for h in heads]
        yield
        kvp = [_bdot_tn(kwp[h], vb[h]) for h in heads]
        yield
        for h in heads:
            og = og_ref[pl.ds(r0, cs), hs[h]]
            h_sv[h] = svp[h]
            h_kv[h] = kvp[h]
            h_q[h] = qb[h]
            h_maxd[h] = jnp.broadcast_to(maxd[h], (cs, LANES))
            h_bcum[h] = bcum_b[h]
            h_row[h] = jnp.concatenate([jnp.broadcast_to(amax[h], (1, LANES)), g_b[h]], axis=0)
            h_gate[h] = mn_ref[:, hs[h]] * _sigmoid(og)

    def stage_b(c):
        r0 = row0(c)
        svp = [h_sv[h] for h in heads]
        kvp = [h_kv[h] for h in heads]
        qb = [h_q[h] for h in heads]
        maxd = [h_maxd[h] for h in heads]
        bcum_b = [h_bcum[h] for h in heads]
        amax = [h_row[h][0:1, :] for h in heads]
        g_b = [h_row[h][1:2, :] for h in heads]
        gate = [h_gate[h] for h in heads]
        c_st = [c_ref[h] for h in heads]
        m_st = [m_ref[h] for h in heads]
        qcn = [_bdot(qb[h], c_st[h]) for h in heads]
        inter = [bcum_b[h] + m_st[h] for h in heads]
        m_row = [jnp.maximum(inter[h], maxd[h]) for h in heads]
        yield
        m_new = [jnp.maximum(g_b[h] + m_st[h], amax[h]) for h in heads]
        for h in heads:
            decay = jnp.exp(g_b[h] + m_st[h] - m_new[h])[:, :1]
            grow = jnp.exp(finite(amax[h]) - m_new[h])[:, :1]
            c_ref[h] = decay * c_st[h] + grow * kvp[h]
            m_ref[h] = m_new[h]
        yield
        f_intra = [jnp.exp(finite(maxd[h]) - m_row[h]) for h in heads]
        w_inter = [jnp.exp(inter[h] - m_row[h]) for h in heads]
        yield
        for h in heads:
            num = f_intra[h] * svp[h][:, :M_DH] + w_inter[h] * qcn[h][:, :M_DH]
            den = f_intra[h] * svp[h][:, M_DH:] + w_inter[h] * qcn[h][:, M_DH:]
            hc = num / jnp.maximum(jnp.abs(den), jnp.exp(-m_row[h]))
            hn = hc * lax.rsqrt(jnp.mean(hc * hc, axis=-1, keepdims=True) + NORM_EPS)
            out_ref[pl.ds(r0, cs), hs[h]] = (hn * gate[h]).astype(BF16)

    def steady(c, carry):
        _run_stages(stage_b(c), stage_a(c + 1))
        return carry

    n_chunks = tt // cs
    _run_stages(stage_a(0))
    lax.fori_loop(0, n_chunks - 1, steady, 0)
    _run_stages(stage_b(n_chunks - 1))


def _mlstm(z, gt_chunks, bias_col, bias_row, m_norm, batch, lp):
    m = z.shape[0]
    tt = LONG_TILE if lp % LONG_TILE == 0 else SEQ_TILE
    nt = lp // tt
    row = lambda b, t: b * nt + t
    gate_col_block = 4 * M_W // LANES + (3 * R_W + 256) // LANES
    handoff = [pltpu.VMEM((M_HEADS, M_CHUNK, 2 * M_DH), F32),
               pltpu.VMEM((M_HEADS, M_DH, 2 * M_DH), F32),
               pltpu.VMEM((M_HEADS, M_CHUNK, M_DH), BF16),
               pltpu.VMEM((M_HEADS, M_CHUNK, LANES), F32),
               pltpu.VMEM((M_HEADS, M_CHUNK, LANES), F32),
               pltpu.VMEM((M_HEADS, 2, LANES), F32),
               pltpu.VMEM((M_HEADS, M_CHUNK, M_DH), F32)]
    in_specs = [
        pl.BlockSpec((tt, M_W), lambda b, t: (row(b, t), 0)),
        pl.BlockSpec((tt, M_W), lambda b, t: (row(b, t), 1)),
        pl.BlockSpec((tt, M_W), lambda b, t: (row(b, t), 2)),
        pl.BlockSpec((tt, M_W), lambda b, t: (row(b, t), 3)),
        pl.BlockSpec((tt, LANES), lambda b, t: (row(b, t), gate_col_block)),
        pl.BlockSpec((tt // M_CHUNK, SUBLANES, M_CHUNK), lambda b, t: (row(b, t), 0, 0)),
        _const_spec((1, LANES)), _const_spec((SUBLANES, M_CHUNK)), _const_spec((1, M_W)),
    ]
    return pl.pallas_call(
        _mlstm_kernel, grid=(batch, nt), in_specs=in_specs,
        out_specs=pl.BlockSpec((tt, M_W), lambda b, t: (row(b, t), 0)),
        out_shape=jax.ShapeDtypeStruct((m, M_W), BF16),
        scratch_shapes=[pltpu.VMEM((M_HEADS, M_DH, 2 * M_DH), F32),
                        pltpu.VMEM((M_HEADS, 1, LANES), F32)] + handoff,
        compiler_params=pltpu.CompilerParams(dimension_semantics=("arbitrary", "arbitrary"),
                                             vmem_limit_bytes=VMEM_LIMIT),
        name="mlstm")(z, z, z, z, z, gt_chunks, bias_col, bias_row, m_norm)


def _rwkv_kernel(r_ref, k_ref, v_ref, wa_ref, g_ref, mur_ref, muk_ref, muv_ref, muwa_ref, mug_ref,
                 w0_ref, w2_ref, a0_ref, a2_ref, g2_ref, kk_ref, ka_ref, rk_ref, lnw_ref, lnb_ref,
                 out_ref, cr_ref, ck_ref, cv_ref, cwa_ref, cg_ref, st_ref,
                 h_left, h_vst, h_end, h_wrb, h_onv, h_apow, h_tinv, h_pend, h_bonus, h_gg):
    t = pl.program_id(1)
    tt = r_ref.shape[0]
    two_c = 2 * CHUNK

    @pl.when(t == 0)
    def _():
        for ref in (cr_ref, ck_ref, cv_ref, cwa_ref, cg_ref, st_ref):
            ref[...] = jnp.zeros_like(ref)

    row_i = lax.broadcasted_iota(jnp.int32, (CHUNK, 1), 0)
    row_8 = lax.broadcasted_iota(jnp.int32, (SUBLANES, 1), 0)
    lo = lax.broadcasted_iota(jnp.int32, (1, LANES), 1) < R_DH
    ri = lax.broadcasted_iota(jnp.int32, (CHUNK, CHUNK), 0)
    ci = lax.broadcasted_iota(jnp.int32, (CHUNK, CHUNK), 1)
    lower = (ci <= ri).astype(BF16)
    r4 = lax.broadcasted_iota(jnp.int32, (4 * CHUNK, 4 * CHUNK), 0)
    c4 = lax.broadcasted_iota(jnp.int32, (4 * CHUNK, 4 * CHUNK), 1)
    same_head = ((r4 // CHUNK) % 2) == ((c4 // CHUNK) % 2)
    below = (c4 % CHUNK < r4 % CHUNK) | ((r4 >= two_c) & (c4 % CHUNK == r4 % CHUNK))
    keep = jnp.where(same_head & below, 1.0, 0.0).astype(BF16)
    r2 = lax.broadcasted_iota(jnp.int32, (two_c, two_c), 0)
    c2 = lax.broadcasted_iota(jnp.int32, (two_c, two_c), 1)
    eye2 = (r2 == c2).astype(F32)
    decay_scale = float(np.exp(-0.5))

    def head_sum(x):
        parts = []
        for p in range(R_PAIRS):
            xp = x[:, p * LANES:(p + 1) * LANES]
            s0 = jnp.sum(jnp.where(lo, xp, 0.0), axis=-1, keepdims=True)
            s1 = jnp.sum(jnp.where(lo, 0.0, xp), axis=-1, keepdims=True)
            parts.append(jnp.where(lo, s0, s1))
        return jnp.concatenate(parts, axis=1)

    def stack_heads(x):
        return jnp.concatenate([jnp.where(lo, x, 0.0), jnp.where(lo, 0.0, x)], axis=0)

    pairs = range(R_PAIRS)
    pair = lambda x, p: x[:, p * LANES:(p + 1) * LANES]
    bf = lambda x: x.astype(BF16)
    row0 = lambda c: c * CHUNK if isinstance(c, int) else pl.multiple_of(c * CHUNK, CHUNK)

    def double(apow, tinv):
        both = [_bdot(apow[p], jnp.concatenate([bf(apow[p]), bf(tinv[p])], axis=1)) for p in pairs]
        return [x[:, :two_c] for x in both], [tinv[p] + both[p][:, two_c:] for p in pairs]

    def stage_a(c):
        r0 = row0(c)

        def shift_mix(x_ref, c_ref, mu_ref):
            x = x_ref[pl.ds(r0, CHUNK), :]
            prev = pltpu.roll(x, 1, axis=0)
            head = jnp.where(row_8 == 0, c_ref[SUBLANES - 1:SUBLANES, :], prev[:SUBLANES])
            prev = jnp.concatenate([head, prev[SUBLANES:]], axis=0)
            c_ref[...] = x[CHUNK - SUBLANES:CHUNK, :]
            return x + (prev - x) * mu_ref[...]

        xwa = shift_mix(wa_ref, cwa_ref, muwa_ref)
        y_w = _bdot(jnp.tanh(xwa), w2_ref[...])
        y_a = _bdot(xwa, a2_ref[...])
        yield
        xg = shift_mix(g_ref, cg_ref, mug_ref)
        gg = _bdot(_sigmoid(xg), g2_ref[...])
        yield
        rr = shift_mix(r_ref, cr_ref, mur_ref)
        yield
        kr = shift_mix(k_ref, ck_ref, muk_ref)
        kk = kr * kk_ref[...]
        yield
        kk = kk * lax.rsqrt(jnp.maximum(head_sum(kk * kk), 1e-24))
        yield
        vr = shift_mix(v_ref, cv_ref, muv_ref)
        yield
        lw = -decay_scale * _sigmoid(w0_ref[...] + y_w)
        cum = _select_dot(lower, lw)
        yield
        aa = _sigmoid(a0_ref[...] + y_a)
        k2 = kr * (1.0 + (aa - 1.0) * ka_ref[...])
        yield
        bonus = head_sum(rr * k2 * rk_ref[...]) * vr
        yield
        e_pos = jnp.exp(cum)
        e_neg = jnp.exp(-cum)
        p_end = e_pos[CHUNK - 1:CHUNK, :]
        yield
        alpha = -kk * jnp.exp(cum - lw)
        rb = rr * e_pos
        yield
        beta = kk * aa * e_neg
        kt = k2 * e_neg
        yield
        beta_end = beta * p_end
        kt_end = kt * p_end
        yield
        left, prod = [], []
        for p in pairs:
            left.append(jnp.concatenate([stack_heads(pair(alpha, p)), stack_heads(pair(rb, p))],
                                        axis=0).astype(BF16))
            right = jnp.concatenate([pair(beta, p)] * 2 + [pair(kt, p)] * 2, axis=0)
            prod.append(_bdot_nt(left[p], right))
            yield
        v_st, end_st = [], []
        for p in pairs:
            v_st.append(stack_heads(pair(vr, p)).astype(BF16))
            end_st.append(jnp.concatenate([stack_heads(pair(beta_end, p)),
                                           stack_heads(pair(kt_end, p))], axis=0).astype(BF16))
            yield
        w_rb, tinv, apow, on_v = [], [], [], []
        for p in pairs:
            masked = prod[p].astype(BF16) * keep
            a_ab = masked[:two_c, :two_c]
            w_rb.append(masked[two_c:, :two_c])
            tinv.append(eye2 + a_ab.astype(F32))
            apow.append(_bdot(a_ab, a_ab))
            on_v.append(_bdot(masked[:, two_c:], v_st[p]))
            yield
        for _ in range(DOUBLINGS_AHEAD):
            apow, tinv = double(apow, tinv)
            yield
        for p in pairs:
            h_left[p] = left[p]
            h_vst[p] = v_st[p]
            h_end[p] = end_st[p]
            h_wrb[p] = w_rb[p]
            h_onv[p] = on_v[p]
            h_apow[p] = bf(apow[p])
            h_tinv[p] = tinv[p]
        h_pend[...] = p_end
        h_bonus[...] = bonus
        h_gg[...] = gg

    def stage_b(c):
        r0 = row0(c)
        left = [h_left[p] for p in pairs]
        v_st = [h_vst[p] for p in pairs]
        end_st = [h_end[p] for p in pairs]
        w_rb = [h_wrb[p] for p in pairs]
        on_v = [h_onv[p] for p in pairs]
        apow = [h_apow[p] for p in pairs]
        tinv = [h_tinv[p] for p in pairs]
        p_end = h_pend[...]
        bonus = h_bonus[...]
        gg = h_gg[...]
        st = [st_ref[p] for p in pairs]
        for _ in range(4 - DOUBLINGS_AHEAD):
            apow, tinv = double(apow, tinv)
            yield
        on_s = [_bdot_nt(left[p], st[p]) for p in pairs]
        tinv = [tinv[p] + _bdot(apow[p], tinv[p]) for p in pairs]
        yield
        u = [_bdot(tinv[p], on_s[p][:two_c] + on_v[p][:two_c]) for p in pairs]
        yield
        o_st = [on_s[p][two_c:] + on_v[p][two_c:] + _bdot(w_rb[p], u[p]) for p in pairs]
        for p in pairs:
            uv = jnp.concatenate([bf(u[p]), v_st[p]], axis=0)
            st_ref[p] = pair(p_end, p) * st[p] + _bdot_tn(uv, end_st[p])
        yield
        o = jnp.concatenate([x[:CHUNK] + x[CHUNK:] for x in o_st], axis=1)
        mean = head_sum(o) * (1.0 / R_DH)
        oc = o - mean
        yield
        var = head_sum(oc * oc) * (1.0 / R_DH)
        y = oc * lax.rsqrt(var + R_LN_EPS) * lnw_ref[...] + lnb_ref[...] + bonus
        valid = (t * tt + r0 + row_i) >= PAD
        out_ref[pl.ds(r0, CHUNK), :] = jnp.where(valid, y * gg, 0.0).astype(BF16)

    def steady(c, carry):
        _run_stages(stage_b(c), stage_a(c + 1), steps=(1, A_STEPS_PER_B_STEP))
        return carry

    n_chunks = tt // CHUNK
    _run_stages(stage_a(0))
    lax.fori_loop(0, n_chunks - 1, steady, 0)
    _run_stages(stage_b(n_chunks - 1))


def _rwkv(z, vecs, mats, batch, lp):
    m = z.shape[0]
    tt = LONG_TILE if lp % LONG_TILE == 0 else SEQ_TILE
    nt = lp // tt
    row = lambda b, t: b * nt + t
    base = 4 * M_W
    two_c = 2 * CHUNK
    handoff = [pltpu.VMEM((R_PAIRS, 2 * two_c, LANES), BF16),
               pltpu.VMEM((R_PAIRS, two_c, LANES), BF16),
               pltpu.VMEM((R_PAIRS, 2 * two_c, LANES), BF16),
               pltpu.VMEM((R_PAIRS, two_c, two_c), BF16),
               pltpu.VMEM((R_PAIRS, 2 * two_c, LANES), F32),
               pltpu.VMEM((R_PAIRS, two_c, two_c), BF16),
               pltpu.VMEM((R_PAIRS, two_c, two_c), F32),
               pltpu.VMEM((1, R_W), F32),
               pltpu.VMEM((CHUNK, R_W), F32),
               pltpu.VMEM((CHUNK, R_W), F32)]
    in_specs = [
        pl.BlockSpec((tt, R_W), lambda b, t: (row(b, t), base // R_W)),
        pl.BlockSpec((tt, R_W), lambda b, t: (row(b, t), base // R_W + 1)),
        pl.BlockSpec((tt, R_W), lambda b, t: (row(b, t), base // R_W + 2)),
        pl.BlockSpec((tt, LANES), lambda b, t: (row(b, t), (base + 3 * R_W) // LANES)),
        pl.BlockSpec((tt, LANES), lambda b, t: (row(b, t), (base + 3 * R_W) // LANES + 1)),
    ]
    mu_r, mu_k, mu_v, mu_wa, mu_g, w0, a0, k_k, k_a, r_k, ln_w, ln_b = vecs
    w2, a2, g2 = mats
    args = [mu_r, mu_k, mu_v, mu_wa, mu_g, w0, w2, a0, a2, g2, k_k, k_a, r_k, ln_w, ln_b]
    in_specs += [_const_spec(a.shape) for a in args]
    return pl.pallas_call(
        _rwkv_kernel, grid=(batch, nt), in_specs=in_specs,
        out_specs=pl.BlockSpec((tt, R_W), lambda b, t: (row(b, t), 0)),
        out_shape=jax.ShapeDtypeStruct((m, R_W), BF16),
        scratch_shapes=[pltpu.VMEM((SUBLANES, R_W), F32), pltpu.VMEM((SUBLANES, R_W), F32),
                        pltpu.VMEM((SUBLANES, R_W), F32), pltpu.VMEM((SUBLANES, LANES), F32),
                        pltpu.VMEM((SUBLANES, LANES), F32),
                        pltpu.VMEM((R_PAIRS, LANES, LANES), F32)] + handoff,
        compiler_params=pltpu.CompilerParams(dimension_semantics=("arbitrary", "arbitrary"),
                                             vmem_limit_bytes=VMEM_LIMIT),
        name="rwkv7")(z, z, z, z, z, *args)


def _retention_kernel(q_ref, k_ref, gate_ref, v_ref, out_ref, st_ref):
    t = pl.program_id(1)
    c = q_ref.shape[0]

    @pl.when(t == 0)
    def _():
        st_ref[...] = jnp.zeros_like(st_ref)

    ri = lax.broadcasted_iota(jnp.int32, (c, c), 0)
    ci = lax.broadcasted_iota(jnp.int32, (c, c), 1)
    causal = jnp.where(ci <= ri, 1.0, 0.0).astype(BF16)
    steps = lax.broadcasted_iota(jnp.int32, (c, 1), 0).astype(F32) + 1.0

    for h in range(T_HEADS):
        log_gamma = float(np.log(1.0 - 2.0 ** (-5.0 - h)))
        ks = slice(h * T_DK, (h + 1) * T_DK)
        vs = slice(h * T_DV, (h + 1) * T_DV)
        q = (q_ref[:, ks] * jnp.exp(log_gamma * steps)).astype(BF16)
        k = (k_ref[:, ks] * (jnp.exp(-log_gamma * steps) * (T_DK ** -0.5))).astype(BF16)
        vb = v_ref[:, vs]
        st = st_ref[h]
        s = _bdot_nt(q, k).astype(BF16) * causal
        o = _bdot(s, vb) + _bdot(q, st)
        st_ref[h] = float(np.exp(log_gamma * c)) * (st + _bdot_tn(k, vb))

        mu = jnp.mean(o, axis=-1, keepdims=True)
        oc = o - mu
        var = jnp.mean(oc * oc, axis=-1, keepdims=True)
        out_ref[:, vs] = (oc * lax.rsqrt(var + NORM_EPS) * gate_ref[:, vs]).astype(BF16)


def _retention(z, v, batch, lp):
    m = z.shape[0]
    tt = SEQ_TILE
    nt = lp // tt
    row = lambda b, t: b * nt + t
    in_specs = [
        pl.BlockSpec((tt, D_MODEL), lambda b, t: (row(b, t), 0)),
        pl.BlockSpec((tt, D_MODEL), lambda b, t: (row(b, t), 1)),
        pl.BlockSpec((tt, T_WV), lambda b, t: (row(b, t), 1)),
        pl.BlockSpec((tt, T_WV), lambda b, t: (row(b, t), 0)),
    ]
    return pl.pallas_call(
        _retention_kernel, grid=(batch, nt), in_specs=in_specs,
        out_specs=pl.BlockSpec((tt, T_WV), lambda b, t: (row(b, t), 0)),
        out_shape=jax.ShapeDtypeStruct((m, T_WV), BF16),
        scratch_shapes=[pltpu.VMEM((T_HEADS, T_DK, T_DV), F32)],
        compiler_params=pltpu.CompilerParams(dimension_semantics=("arbitrary", "arbitrary"),
                                             vmem_limit_bytes=VMEM_LIMIT),
        name="retention")(z, z, z, v)


def _ffn_kernel(n_mix, last, *refs):
    h_ref = refs[0]
    a_refs = refs[1:1 + n_mix]
    wo_refs = refs[1 + n_mix:1 + 2 * n_mix]
    g_ref, wv_ref, wg_ref, cw_ref, cb_ref, wd_ref = refs[1 + 2 * n_mix:7 + 2 * n_mix]
    rest = refs[7 + 2 * n_mix:]
    if last:
        gf_ref, hm_ref = rest[:2]
        am_refs = rest[2:2 + n_mix]
        rest = rest[2 + n_mix:]
    o_ref, carry_ref, hn_ref, act_ref = rest
    tm = h_ref.shape[0]

    def mixer_out(x_ref, x_refs):
        mixed = x_ref[...]
        for a_ref, wo_ref in zip(x_refs, wo_refs):
            mixed = mixed + jnp.dot(a_ref[...], wo_ref[...], preferred_element_type=F32)
        return mixed

    @pl.when(pl.program_id(1 if last else 0) == 0)
    def _():
        if last:
            meta_hn = _rmsnorm_bf16(mixer_out(hm_ref, am_refs), g_ref[...])
            meta_gate = jnp.dot(meta_hn, wg_ref[...], preferred_element_type=F32)
            carry_ref[...] = meta_gate[N_META - SUBLANES:, :]
        else:
            carry_ref[...] = jnp.zeros_like(carry_ref)

    mixed = mixer_out(h_ref, a_refs)
    hn_ref[...] = _rmsnorm_bf16(mixed, g_ref[...])
    row = lax.broadcasted_iota(jnp.int32, (SUBLANES, 1), 0)

    for c in range(N_FF_CHUNKS):
        cs = slice(c * FF_CHUNK, (c + 1) * FF_CHUNK)
        hn = hn_ref[...]
        val = jnp.dot(hn, wv_ref[:, cs], preferred_element_type=F32)
        gate = jnp.dot(hn, wg_ref[:, cs], preferred_element_type=F32)
        tail = carry_ref[:, cs]
        prev1 = pltpu.roll(gate, 1, axis=0)
        prev2 = pltpu.roll(gate, 2, axis=0)
        head1 = jnp.where(row == 0, tail[SUBLANES - 1:SUBLANES, :], prev1[:SUBLANES])
        head2 = jnp.where(row == 0, tail[SUBLANES - 2:SUBLANES - 1, :], prev2[:SUBLANES])
        head2 = jnp.where(row == 1, tail[SUBLANES - 1:SUBLANES, :], head2)
        prev1 = jnp.concatenate([head1, prev1[SUBLANES:]], axis=0)
        prev2 = jnp.concatenate([head2, prev2[SUBLANES:]], axis=0)
        carry_ref[:, cs] = gate[tm - SUBLANES:tm, :]
        conv = (cw_ref[2:3, cs] * gate + cw_ref[1:2, cs] * prev1 + cw_ref[0:1, cs] * prev2
                + cb_ref[:, cs])
        act_ref[:, cs] = (conv * _sigmoid(conv) * val).astype(BF16)
    out = mixed + jnp.dot(act_ref[...], wd_ref[...], preferred_element_type=F32)
    if last:
        ms = jnp.mean(out * out, axis=-1, keepdims=True)
        o_ref[0] = out * lax.rsqrt(ms + NORM_EPS) * gf_ref[...]
    else:
        o_ref[...] = out


def _mix_ffn(h, acts, w_outs, g, wv, wg, cw, cb, wd):
    m, d = h.shape
    tm = ROW_TILE
    in_specs = [pl.BlockSpec((tm, d), lambda i: (i, 0))]
    in_specs += [pl.BlockSpec((tm, a.shape[1]), lambda i: (i, 0)) for a in acts]
    in_specs += [_const_spec(a.shape) for a in (*w_outs, g, wv, wg, cw, cb, wd)]
    return pl.pallas_call(
        functools.partial(_ffn_kernel, len(acts), False), grid=(m // tm,), in_specs=in_specs,
        out_specs=pl.BlockSpec((tm, d), lambda i: (i, 0)),
        out_shape=jax.ShapeDtypeStruct((m, d), F32),
        scratch_shapes=[pltpu.VMEM((SUBLANES, D_FF), F32),
                        pltpu.VMEM((tm, d), BF16),
                        pltpu.VMEM((tm, D_FF), BF16)],
        compiler_params=pltpu.CompilerParams(dimension_semantics=("arbitrary",),
                                             vmem_limit_bytes=VMEM_LIMIT),
        name="mix_ffn")(h, *acts, *w_outs, g, wv, wg, cw, cb, wd)


def _last_mix_ffn(h, acts, w_outs, g, wv, wg, cw, cb, wd, g_final, batch, lp):
    d = h.shape[1]
    seq = lp - FRONT
    tm = ROW_TILE if seq % ROW_TILE == 0 else FRONT
    tile = lambda cols: pl.BlockSpec(
        (pl.Element(tm), pl.Element(cols)),
        lambda b, j: (pl.multiple_of(b * lp + FRONT + j * tm, FRONT), 0))
    meta = lambda cols: pl.BlockSpec(
        (pl.Element(N_META), pl.Element(cols)),
        lambda b, j: (pl.multiple_of(b * lp + PAD, N_META), 0))
    in_specs = [tile(d)] + [tile(a.shape[1]) for a in acts]
    in_specs += [_const_spec(a.shape) for a in (*w_outs, g, wv, wg, cw, cb, wd, g_final)]
    in_specs += [meta(d)] + [meta(a.shape[1]) for a in acts]
    return pl.pallas_call(
        functools.partial(_ffn_kernel, len(acts), True), grid=(batch, seq // tm),
        in_specs=in_specs,
        out_specs=pl.BlockSpec((1, tm, d), lambda b, j: (b, j, 0)),
        out_shape=jax.ShapeDtypeStruct((batch, seq, d), F32),
        scratch_shapes=[pltpu.VMEM((SUBLANES, D_FF), F32),
                        pltpu.VMEM((tm, d), BF16),
                        pltpu.VMEM((tm, D_FF), BF16)],
        compiler_params=pltpu.CompilerParams(dimension_semantics=("arbitrary", "arbitrary"),
                                             vmem_limit_bytes=VMEM_LIMIT),
        name="last_mix_ffn")(h, *acts, *w_outs, g, wv, wg, cw, cb, wd, g_final, h, *acts)


def _ffn_weights(w_up, conv_w, conv_b, w_down):
    wv = w_up[:, :D_FF].astype(BF16)
    wg = w_up[:, D_FF:].astype(BF16)
    cw = jnp.pad(conv_w, ((0, SUBLANES - conv_w.shape[0]), (0, 0)))
    return wv, wg, cw, conv_b.reshape(1, D_FF), w_down.astype(BF16)


def kernel(x, meta_tokens, norm_mix, norm_ffn, norm_final, e_w_in, e_w_out, m_b_i, m_b_f, m_norm,
           r_mu, r_w0, r_w2, r_a0, r_a2, r_g2, r_k_k, r_k_a, r_r_k, r_ln_w, r_ln_b, o_w_in, o_w_out,
           f_w_up, f_conv_w, f_conv_b, f_w_down):
    batch, seq, d = x.shape
    lp = seq + FRONT
    assert d == D_MODEL and lp % SEQ_TILE == 0 and (batch * lp) % ROW_TILE == 0
    m = batch * lp
    row = lambda v: v.reshape(1, -1).astype(F32)

    meta = jnp.broadcast_to(meta_tokens[None].astype(x.dtype), (batch, N_META, d))
    h = jnp.concatenate([jnp.zeros((batch, PAD, d), x.dtype), meta, x], axis=1).reshape(m, d)

    w_in = e_w_in[0]
    n_m = 4 * M_W
    gates_w = w_in[:, n_m:n_m + 2 * M_HEADS]
    w0 = jnp.concatenate([w_in[:, :n_m], w_in[:, n_m + 2 * M_HEADS:], gates_w,
                          jnp.zeros((d, LANES - 2 * M_HEADS), F32)], axis=1).astype(BF16)
    z, gt = _norm_proj(h, row(norm_mix[0]), w0, gates_w.T.astype(BF16))
    gt_chunks = gt.reshape(SUBLANES, m // M_CHUNK, M_CHUNK).transpose(1, 0, 2)
    gate_bias = jnp.concatenate([m_b_i[0], m_b_f[0]])
    bias_col = jnp.pad(gate_bias, (0, LANES - 2 * M_HEADS)).reshape(1, LANES)
    bias_row = jnp.broadcast_to(gate_bias[:, None], (SUBLANES, M_CHUNK))
    mix_m = _mlstm(z, gt_chunks, bias_col, bias_row, row(m_norm[0]), batch, lp)

    mu = r_mu[0]
    rank_wa = r_w2.shape[1] + r_a2.shape[1]
    vecs = [row(mu[:R_W]), row(mu[R_W:2 * R_W]), row(mu[2 * R_W:3 * R_W]),
            row(mu[3 * R_W:3 * R_W + rank_wa]), row(mu[3 * R_W + rank_wa:]),
            row(r_w0[0]), row(r_a0[0]), row(r_k_k[0]), row(r_k_a[0]), row(r_r_k[0]),
            row(r_ln_w[0]), row(r_ln_b[0])]
    w2 = jnp.concatenate([r_w2[0], jnp.zeros_like(r_a2[0])], axis=0).astype(BF16)
    a2 = jnp.concatenate([jnp.zeros_like(r_w2[0]), r_a2[0]], axis=0).astype(BF16)
    mix_r = _rwkv(z, vecs, [w2, a2, r_g2[0].astype(BF16)], batch, lp)

    w_out = e_w_out[0].astype(BF16)
    h = _mix_ffn(h, [mix_m, mix_r], [w_out[:M_W], w_out[M_W:]], row(norm_ffn[0]),
                 *_ffn_weights(f_w_up[0], f_conv_w[0], f_conv_b[0], f_w_down[0]))

    w_in = o_w_in[0]
    perm = np.concatenate([np.arange(0, T_DK, 2), np.arange(1, T_DK, 2)])
    qk_cols = np.concatenate([hh * T_DK + perm for hh in range(T_HEADS)])
    w1 = jnp.concatenate([w_in[:, qk_cols], w_in[:, D_MODEL + qk_cols],
                          w_in[:, 2 * D_MODEL + T_WV:], w_in[:, 2 * D_MODEL:2 * D_MODEL + T_WV]],
                         axis=1).astype(BF16)
    inv = 1.0 / (ROPE_BASE ** jnp.linspace(0.0, 1.0, T_DK // 2, dtype=F32))
    pos = jnp.arange(lp, dtype=F32) - PAD
    ang = pos[:, None] * inv[None, :]
    cos = jnp.tile(jnp.concatenate([jnp.cos(ang), jnp.cos(ang)], axis=1), (batch, 1))
    sin = jnp.tile(jnp.concatenate([-jnp.sin(ang), jnp.sin(ang)], axis=1), (batch, 1))
    z, v = _norm_proj(h, row(norm_mix[1]), w1, rotary=(cos, sin))
    o = _retention(z, v, batch, lp)
    return _last_mix_ffn(h, [o], [o_w_out[0].astype(BF16)], row(norm_ffn[1]),
                         *_ffn_weights(f_w_up[1], f_conv_w[1], f_conv_b[1], f_w_down[1]),
                         row(norm_final), batch, lp)
```

```python
import functools

import numpy as np
import jax
import jax.numpy as jnp
from jax import lax
from jax.experimental import pallas as pl
from jax.experimental.pallas import tpu as pltpu

F32 = jnp.float32
BF16 = jnp.bfloat16

D_MODEL = 1024
N_META = 16
NORM_EPS = 1e-6
M_HEADS = 4
M_DH = 128
M_W = M_HEADS * M_DH
GATE_CAP = 15.0
R_DH = 64
R_HEADS = 8
R_W = R_HEADS * R_DH
R_PAIRS = R_HEADS // 2
R_LN_EPS = 64e-5
T_HEADS = 4
T_DK = 256
T_DV = 512
T_WV = T_HEADS * T_DV
ROPE_BASE = 10000.0
D_FF = 2816
FF_CHUNK = 256
N_FF_CHUNKS = D_FF // FF_CHUNK

LANES = 128
SUBLANES = 8
FRONT = 128
PAD = FRONT - N_META
CHUNK = 64
M_CHUNK = 128
ROW_TILE = 512
SEQ_TILE = 384
LONG_TILE = 1408
DOUBLINGS_AHEAD = 1
A_STEPS_PER_B_STEP = 5
VMEM_LIMIT = 56 * 1024 * 1024


def _const_spec(shape):
    nd = len(shape)
    return pl.BlockSpec(shape, lambda *_: (0,) * nd, pipeline_mode=pl.Buffered(1))


def _sigmoid(x):
    return 1.0 / (1.0 + jnp.exp(-x))


def _softplus(x):
    return jnp.maximum(x, 0.0) + jnp.log1p(jnp.exp(-jnp.abs(x)))


def _bdot(a, b):
    return jnp.dot(a.astype(BF16), b.astype(BF16), preferred_element_type=F32)


def _bdot_nt(a, b):
    return lax.dot_general(a.astype(BF16), b.astype(BF16), (((1,), (1,)), ((), ())),
                           preferred_element_type=F32)


def _bdot_tn(a, b):
    return lax.dot_general(a.astype(BF16), b.astype(BF16), (((0,), (0,)), ((), ())),
                           preferred_element_type=F32)


def _split3(x):
    hi = x.astype(BF16)
    rest = x - hi.astype(F32)
    mid = rest.astype(BF16)
    return hi, mid, (rest - mid.astype(F32)).astype(BF16)


def _select_dot(sel, x):
    hi, mid, low = _split3(x)
    dot = lambda y: jnp.dot(sel, y, preferred_element_type=F32)
    return dot(hi) + dot(mid) + dot(low)


def _dot_select(x, sel):
    hi, mid, low = _split3(x)
    dot = lambda y: jnp.dot(y, sel, preferred_element_type=F32)
    return dot(hi) + dot(mid) + dot(low)


def _run_stages(*stages, steps=None):
    steps = steps or (1,) * len(stages)
    live = list(zip(stages, steps))
    done = object()
    while live:
        live = [(s, n) for s, n in live if all(next(s, done) is not done for _ in range(n))]


def _rmsnorm_bf16(x, g):
    ms = jnp.mean(x * x, axis=-1, keepdims=True)
    return (x * lax.rsqrt(ms + NORM_EPS) * g).astype(BF16)


def _col_chunks(n, width=512):
    return [(c, min(width, n - c)) for c in range(0, n, width)]


def _norm_proj_kernel(h_ref, g_ref, w_ref, o_ref, hn_ref):
    hn_ref[...] = _rmsnorm_bf16(h_ref[...], g_ref[...])
    for c0, cw in _col_chunks(w_ref.shape[1]):
        o_ref[:, c0:c0 + cw] = jnp.dot(hn_ref[...], w_ref[:, c0:c0 + cw],
                                       preferred_element_type=F32)


def _norm_proj_gates_kernel(h_ref, g_ref, w_ref, wgt_ref, o_ref, gt_ref, hn_ref):
    _norm_proj_kernel(h_ref, g_ref, w_ref, o_ref, hn_ref)
    for c in range(gt_ref.shape[0]):
        gt_ref[c] = lax.dot_general(wgt_ref[...], hn_ref[c * M_CHUNK:(c + 1) * M_CHUNK, :],
                                    (((1,), (1,)), ((), ())), preferred_element_type=F32)


def _norm_proj_retention_kernel(h_ref, g_ref, w_ref, cos_ref, sin_ref, o_ref, ob_ref, hn_ref):
    hn_ref[...] = _rmsnorm_bf16(h_ref[...], g_ref[...])
    n_f32 = o_ref.shape[1]
    half = T_DK // 2
    for c0, cw in _col_chunks(w_ref.shape[1], 2 * T_DK):
        z = jnp.dot(hn_ref[...], w_ref[:, c0:c0 + cw], preferred_element_type=F32)
        if c0 < 2 * D_MODEL:
            cos = cos_ref[...]
            sin = sin_ref[...]
            for j in range(0, cw, T_DK):
                x = z[:, j:j + T_DK]
                swapped = jnp.concatenate([x[:, half:], x[:, :half]], axis=1)
                o_ref[:, c0 + j:c0 + j + T_DK] = x * cos + swapped * sin
        elif c0 < n_f32:
            o_ref[:, c0:c0 + cw] = z * _sigmoid(z)
        else:
            ob_ref[:, c0 - n_f32:c0 - n_f32 + cw] = z.astype(BF16)


def _norm_proj(h, g, w, wgt=None, rotary=None):
    m, d = h.shape
    n = w.shape[1]
    tm = ROW_TILE
    in_specs = [pl.BlockSpec((tm, d), lambda i: (i, 0)), _const_spec((1, d)), _const_spec((d, n))]
    params = pltpu.CompilerParams(dimension_semantics=("arbitrary",), vmem_limit_bytes=VMEM_LIMIT)
    if wgt is None:
        n_f32 = n - T_WV
        table = pl.BlockSpec((tm, T_DK), lambda i: (i, 0))
        return pl.pallas_call(
            _norm_proj_retention_kernel, grid=(m // tm,), in_specs=in_specs + [table, table],
            out_specs=[pl.BlockSpec((tm, n_f32), lambda i: (i, 0)),
                       pl.BlockSpec((tm, T_WV), lambda i: (i, 0))],
            out_shape=[jax.ShapeDtypeStruct((m, n_f32), F32),
                       jax.ShapeDtypeStruct((m, T_WV), BF16)],
            scratch_shapes=[pltpu.VMEM((tm, d), BF16)],
            compiler_params=params, name="norm_proj")(h, g, w, *rotary)
    return pl.pallas_call(
        _norm_proj_gates_kernel, grid=(m // tm,),
        in_specs=in_specs + [_const_spec(wgt.shape)],
        out_specs=[pl.BlockSpec((tm, n), lambda i: (i, 0)),
                   pl.BlockSpec((tm // M_CHUNK, SUBLANES, M_CHUNK), lambda i: (i, 0, 0))],
        out_shape=[jax.ShapeDtypeStruct((m, n), F32),
                   jax.ShapeDtypeStruct((m // M_CHUNK, SUBLANES, M_CHUNK), F32)],
        scratch_shapes=[pltpu.VMEM((tm, d), BF16)],
        compiler_params=params, name="norm_proj_gates")(h, g, w, wgt)


def _gate_act(z, is_input_gate):
    capped = GATE_CAP * jnp.tanh(z / GATE_CAP)
    return jnp.where(is_input_gate, capped, -_softplus(-capped))


def _mlstm_kernel(q_ref, k_ref, v_ref, og_ref, gc_ref, gr_ref, bc_ref, br_ref, mn_ref,
                  out_ref, c_ref, m_ref, h_sv, h_kv, h_q, h_maxd, h_bcum, h_row, h_gate):
    t = pl.program_id(1)
    tt = q_ref.shape[0]

    @pl.when(t == 0)
    def _():
        c_ref[...] = jnp.zeros_like(c_ref)
        m_ref[...] = jnp.zeros_like(m_ref)

    cs = M_CHUNK
    ri = lax.broadcasted_iota(jnp.int32, (cs, cs), 0)
    ci = lax.broadcasted_iota(jnp.int32, (cs, cs), 1)
    causal = ci <= ri
    lower = jnp.where(causal, 1.0, 0.0).astype(BF16)
    upper = jnp.where(ri <= ci, 1.0, 0.0).astype(BF16)
    n_blocks = 2 * M_HEADS
    sel_r = lax.broadcasted_iota(jnp.int32, (LANES, n_blocks * LANES), 0)
    sel_c = lax.broadcasted_iota(jnp.int32, (LANES, n_blocks * LANES), 1)
    spread = jnp.where(sel_r == sel_c // LANES, 1.0, 0.0).astype(BF16)
    ones_blk = jnp.ones((cs, LANES), BF16)
    col_lane = lax.broadcasted_iota(jnp.int32, (1, LANES), 1)
    col_row = lax.broadcasted_iota(jnp.int32, (cs, 1), 0)
    row_sub = lax.broadcasted_iota(jnp.int32, (SUBLANES, 1), 0)
    row_lane = lax.broadcasted_iota(jnp.int32, (1, cs), 1)
    scale = M_DH ** -0.5
    heads = range(M_HEADS)
    hs = [slice(h * M_DH, (h + 1) * M_DH) for h in heads]
    blk = lambda x, j: x[:, j * LANES:(j + 1) * LANES]
    row0 = lambda c: c * cs if isinstance(c, int) else pl.multiple_of(c * cs, cs)
    finite = lambda x: jnp.where(x == -jnp.inf, 0.0, x)

    def stage_a(c):
        r0 = row0(c)
        pos0 = t * tt + r0
        gact = _gate_act(gc_ref[pl.ds(r0, cs), :] + bc_ref[...], col_lane < M_HEADS)
        valid = pos0 + col_row >= PAD
        bcum = _select_dot(lower, jnp.where(valid, gact, 0.0))
        spread_g = _dot_select(jnp.where(col_lane < M_HEADS, gact, bcum), spread)
        li_b = [jnp.where(valid, blk(spread_g, h), -jnp.inf) for h in heads]
        bcum_b = [blk(spread_g, M_HEADS + h) for h in heads]
        gr = _gate_act(gr_ref[c] + br_ref[...], row_sub < M_HEADS)
        gr = jnp.where(pos0 + row_lane >= PAD, gr, jnp.where(row_sub < M_HEADS, -jnp.inf, 0.0))
        bcum_r = _dot_select(jnp.where(row_sub < M_HEADS, 0.0, gr), upper)
        yield
        qb = [q_ref[pl.ds(r0, cs), hs[h]].astype(BF16) for h in heads]
        k = [k_ref[pl.ds(r0, cs), hs[h]] * scale for h in heads]
        vb = [jnp.concatenate([v_ref[pl.ds(r0, cs), hs[h]].astype(BF16), ones_blk], axis=1)
              for h in heads]
        qk = [_bdot_nt(qb[h], k[h]) for h in heads]
        yield
        brow = [bcum_r[M_HEADS + h:M_HEADS + h + 1, :] for h in heads]
        lirow = [gr[h:h + 1, :] for h in heads]
        g_b = [bcum_b[h][cs - 1:cs, :] for h in heads]
        dmat = [jnp.where(causal, bcum_b[h][:, :cs] - brow[h] + lirow[h], -jnp.inf)
                for h in heads]
        maxd = [jnp.max(dmat[h], axis=-1, keepdims=True) for h in heads]
        a_row = [g_b[h][:, :1] - brow[h] + lirow[h] for h in heads]
        amax = [jnp.max(a_row[h], axis=-1, keepdims=True) for h in heads]
        yield
        sp = [qk[h] * jnp.exp(dmat[h] - finite(maxd[h])) for h in heads]
        kwp = [k[h] * jnp.exp(g_b[h] - bcum_b[h] + li_b[h] - finite(amax[h])) for h in heads]
        svp = [_bdot(sp[h], vb[h]) for h in heads]
        yield
        kvp = [_bdot_tn(kwp[h], vb[h]) for h in heads]
        yield
        for h in heads:
            og = og_ref[pl.ds(r0, cs), hs[h]]
            h_sv[h] = svp[h]
            h_kv[h] = kvp[h]
            h_q[h] = qb[h]
            h_maxd[h] = jnp.broadcast_to(maxd[h], (cs, LANES))
            h_bcum[h] = bcum_b[h]
            h_row[h] = jnp.concatenate([jnp.broadcast_to(amax[h], (1, LANES)), g_b[h]], axis=0)
            h_gate[h] = mn_ref[:, hs[h]] * _sigmoid(og)

    def stage_b(c):
        r0 = row0(c)
        svp = [h_sv[h] for h in heads]
        kvp = [h_kv[h] for h in heads]
        qb = [h_q[h] for h in heads]
        maxd = [h_maxd[h] for h in heads]
        bcum_b = [h_bcum[h] for h in heads]
        amax = [h_row[h][0:1, :] for h in heads]
        g_b = [h_row[h][1:2, :] for h in heads]
        gate = [h_gate[h] for h in heads]
        c_st = [c_ref[h] for h in heads]
        m_st = [m_ref[h] for h in heads]
        qcn = [_bdot(qb[h], c_st[h]) for h in heads]
        inter = [bcum_b[h] + m_st[h] for h in heads]
        m_row = [jnp.maximum(inter[h], maxd[h]) for h in heads]
        yield
        m_new = [jnp.maximum(g_b[h] + m_st[h], amax[h]) for h in heads]
        for h in heads:
            decay = jnp.exp(g_b[h] + m_st[h] - m_new[h])[:, :1]
            grow = jnp.exp(finite(amax[h]) - m_new[h])[:, :1]
            c_ref[h] = decay * c_st[h] + grow * kvp[h]
            m_ref[h] = m_new[h]
        yield
        f_intra = [jnp.exp(finite(maxd[h]) - m_row[h]) for h in heads]
        w_inter = [jnp.exp(inter[h] - m_row[h]) for h in heads]
        yield
        for h in heads:
            num = f_intra[h] * svp[h][:, :M_DH] + w_inter[h] * qcn[h][:, :M_DH]
            den = f_intra[h] * svp[h][:, M_DH:] + w_inter[h] * qcn[h][:, M_DH:]
            hc = num / jnp.maximum(jnp.abs(den), jnp.exp(-m_row[h]))
            hn = hc * lax.rsqrt(jnp.mean(hc * hc, axis=-1, keepdims=True) + NORM_EPS)
            out_ref[pl.ds(r0, cs), hs[h]] = (hn * gate[h]).astype(BF16)

    def steady(c, carry):
        _run_stages(stage_b(c), stage_a(c + 1))
        return carry

    n_chunks = tt // cs
    _run_stages(stage_a(0))
    lax.fori_loop(0, n_chunks - 1, steady, 0)
    _run_stages(stage_b(n_chunks - 1))


def _mlstm(z, gt_chunks, bias_col, bias_row, m_norm, batch, lp):
    m = z.shape[0]
    tt = LONG_TILE if lp % LONG_TILE == 0 else SEQ_TILE
    nt = lp // tt
    row = lambda b, t: b * nt + t
    gate_col_block = 4 * M_W // LANES + (3 * R_W + 256) // LANES
    handoff = [pltpu.VMEM((M_HEADS, M_CHUNK, 2 * M_DH), F32),
               pltpu.VMEM((M_HEADS, M_DH, 2 * M_DH), F32),
               pltpu.VMEM((M_HEADS, M_CHUNK, M_DH), BF16),
               pltpu.VMEM((M_HEADS, M_CHUNK, LANES), F32),
               pltpu.VMEM((M_HEADS, M_CHUNK, LANES), F32),
               pltpu.VMEM((M_HEADS, 2, LANES), F32),
               pltpu.VMEM((M_HEADS, M_CHUNK, M_DH), F32)]
    in_specs = [
        pl.BlockSpec((tt, M_W), lambda b, t: (row(b, t), 0)),
        pl.BlockSpec((tt, M_W), lambda b, t: (row(b, t), 1)),
        pl.BlockSpec((tt, M_W), lambda b, t: (row(b, t), 2)),
        pl.BlockSpec((tt, M_W), lambda b, t: (row(b, t), 3)),
        pl.BlockSpec((tt, LANES), lambda b, t: (row(b, t), gate_col_block)),
        pl.BlockSpec((tt // M_CHUNK, SUBLANES, M_CHUNK), lambda b, t: (row(b, t), 0, 0)),
        _const_spec((1, LANES)), _const_spec((SUBLANES, M_CHUNK)), _const_spec((1, M_W)),
    ]
    return pl.pallas_call(
        _mlstm_kernel, grid=(batch, nt), in_specs=in_specs,
        out_specs=pl.BlockSpec((tt, M_W), lambda b, t: (row(b, t), 0)),
        out_shape=jax.ShapeDtypeStruct((m, M_W), BF16),
        scratch_shapes=[pltpu.VMEM((M_HEADS, M_DH, 2 * M_DH), F32),
                        pltpu.VMEM((M_HEADS, 1, LANES), F32)] + handoff,
        compiler_params=pltpu.CompilerParams(dimension_semantics=("arbitrary", "arbitrary"),
                                             vmem_limit_bytes=VMEM_LIMIT),
        name="mlstm")(z, z, z, z, z, gt_chunks, bias_col, bias_row, m_norm)


def _rwkv_kernel(r_ref, k_ref, v_ref, wa_ref, g_ref, mur_ref, muk_ref, muv_ref, muwa_ref, mug_ref,
                 w0_ref, w2_ref, a0_ref, a2_ref, g2_ref, kk_ref, ka_ref, rk_ref, lnw_ref, lnb_ref,
                 out_ref, cr_ref, ck_ref, cv_ref, cwa_ref, cg_ref, st_ref,
                 h_left, h_vst, h_end, h_wrb, h_onv, h_apow, h_tinv, h_pend, h_bonus, h_gg):
    t = pl.program_id(1)
    tt = r_ref.shape[0]
    two_c = 2 * CHUNK

    @pl.when(t == 0)
    def _():
        for ref in (cr_ref, ck_ref, cv_ref, cwa_ref, cg_ref, st_ref):
            ref[...] = jnp.zeros_like(ref)

    row_i = lax.broadcasted_iota(jnp.int32, (CHUNK, 1), 0)
    row_8 = lax.broadcasted_iota(jnp.int32, (SUBLANES, 1), 0)
    lo = lax.broadcasted_iota(jnp.int32, (1, LANES), 1) < R_DH
    ri = lax.broadcasted_iota(jnp.int32, (CHUNK, CHUNK), 0)
    ci = lax.broadcasted_iota(jnp.int32, (CHUNK, CHUNK), 1)
    lower = (ci <= ri).astype(BF16)
    r4 = lax.broadcasted_iota(jnp.int32, (4 * CHUNK, 4 * CHUNK), 0)
    c4 = lax.broadcasted_iota(jnp.int32, (4 * CHUNK, 4 * CHUNK), 1)
    same_head = ((r4 // CHUNK) % 2) == ((c4 // CHUNK) % 2)
    below = (c4 % CHUNK < r4 % CHUNK) | ((r4 >= two_c) & (c4 % CHUNK == r4 % CHUNK))
    keep = jnp.where(same_head & below, 1.0, 0.0).astype(BF16)
    r2 = lax.broadcasted_iota(jnp.int32, (two_c, two_c), 0)
    c2 = lax.broadcasted_iota(jnp.int32, (two_c, two_c), 1)
    eye2 = (r2 == c2).astype(F32)
    decay_scale = float(np.exp(-0.5))

    def head_sum(x):
        parts = []
        for p in range(R_PAIRS):
            xp = x[:, p * LANES:(p + 1) * LANES]
            s0 = jnp.sum(jnp.where(lo, xp, 0.0), axis=-1, keepdims=True)
            s1 = jnp.sum(jnp.where(lo, 0.0, xp), axis=-1, keepdims=True)
            parts.append(jnp.where(lo, s0, s1))
        return jnp.concatenate(parts, axis=1)

    def stack_heads(x):
        return jnp.concatenate([jnp.where(lo, x, 0.0), jnp.where(lo, 0.0, x)], axis=0)

    pairs = range(R_PAIRS)
    pair = lambda x, p: x[:, p * LANES:(p + 1) * LANES]
    bf = lambda x: x.astype(BF16)
    row0 = lambda c: c * CHUNK if isinstance(c, int) else pl.multiple_of(c * CHUNK, CHUNK)

    def double(apow, tinv):
        both = [_bdot(apow[p], jnp.concatenate([bf(apow[p]), bf(tinv[p])], axis=1)) for p in pairs]
        return [x[:, :two_c] for x in both], [tinv[p] + both[p][:, two_c:] for p in pairs]

    def stage_a(c):
        r0 = row0(c)

        def shift_mix(x_ref, c_ref, mu_ref):
            x = x_ref[pl.ds(r0, CHUNK), :]
            prev = pltpu.roll(x, 1, axis=0)
            head = jnp.where(row_8 == 0, c_ref[SUBLANES - 1:SUBLANES, :], prev[:SUBLANES])
            prev = jnp.concatenate([head, prev[SUBLANES:]], axis=0)
            c_ref[...] = x[CHUNK - SUBLANES:CHUNK, :]
            return x + (prev - x) * mu_ref[...]

        xwa = shift_mix(wa_ref, cwa_ref, muwa_ref)
        y_w = _bdot(jnp.tanh(xwa), w2_ref[...])
        y_a = _bdot(xwa, a2_ref[...])
        yield
        xg = shift_mix(g_ref, cg_ref, mug_ref)
        gg = _bdot(_sigmoid(xg), g2_ref[...])
        yield
        rr = shift_mix(r_ref, cr_ref, mur_ref)
        yield
        kr = shift_mix(k_ref, ck_ref, muk_ref)
        kk = kr * kk_ref[...]
        yield
        kk = kk * lax.rsqrt(jnp.maximum(head_sum(kk * kk), 1e-24))
        yield
        vr = shift_mix(v_ref, cv_ref, muv_ref)
        yield
        lw = -decay_scale * _sigmoid(w0_ref[...] + y_w)
        cum = _select_dot(lower, lw)
        yield
        aa = _sigmoid(a0_ref[...] + y_a)
        k2 = kr * (1.0 + (aa - 1.0) * ka_ref[...])
        yield
        bonus = head_sum(rr * k2 * rk_ref[...]) * vr
        yield
        e_pos = jnp.exp(cum)
        e_neg = jnp.exp(-cum)
        p_end = e_pos[CHUNK - 1:CHUNK, :]
        yield
        alpha = -kk * jnp.exp(cum - lw)
        rb = rr * e_pos
        yield
        beta = kk * aa * e_neg
        kt = k2 * e_neg
        yield
        beta_end = beta * p_end
        kt_end = kt * p_end
        yield
        left, prod = [], []
        for p in pairs:
            left.append(jnp.concatenate([stack_heads(pair(alpha, p)), stack_heads(pair(rb, p))],
                                        axis=0).astype(BF16))
            right = jnp.concatenate([pair(beta, p)] * 2 + [pair(kt, p)] * 2, axis=0)
            prod.append(_bdot_nt(left[p], right))
            yield
        v_st, end_st = [], []
        for p in pairs:
            v_st.append(stack_heads(pair(vr, p)).astype(BF16))
            end_st.append(jnp.concatenate([stack_heads(pair(beta_end, p)),
                                           stack_heads(pair(kt_end, p))], axis=0).astype(BF16))
            yield
        w_rb, tinv, apow, on_v = [], [], [], []
        for p in pairs:
            masked = prod[p].astype(BF16) * keep
            a_ab = masked[:two_c, :two_c]
            w_rb.append(masked[two_c:, :two_c])
            tinv.append(eye2 + a_ab.astype(F32))
            apow.append(_bdot(a_ab, a_ab))
            on_v.append(_bdot(masked[:, two_c:], v_st[p]))
            yield
        for _ in range(DOUBLINGS_AHEAD):
            apow, tinv = double(apow, tinv)
            yield
        for p in pairs:
            h_left[p] = left[p]
            h_vst[p] = v_st[p]
            h_end[p] = end_st[p]
            h_wrb[p] = w_rb[p]
            h_onv[p] = on_v[p]
            h_apow[p] = bf(apow[p])
            h_tinv[p] = tinv[p]
        h_pend[...] = p_end
        h_bonus[...] = bonus
        h_gg[...] = gg

    def stage_b(c):
        r0 = row0(c)
        left = [h_left[p] for p in pairs]
        v_st = [h_vst[p] for p in pairs]
        end_st = [h_end[p] for p in pairs]
        w_rb = [h_wrb[p] for p in pairs]
        on_v = [h_onv[p] for p in pairs]
        apow = [h_apow[p] for p in pairs]
        tinv = [h_tinv[p] for p in pairs]
        p_end = h_pend[...]
        bonus = h_bonus[...]
        gg = h_gg[...]
        st = [st_ref[p] for p in pairs]
        for _ in range(4 - DOUBLINGS_AHEAD):
            apow, tinv = double(apow, tinv)
            yield
        on_s = [_bdot_nt(left[p], st[p]) for p in pairs]
        tinv = [tinv[p] + _bdot(apow[p], tinv[p]) for p in pairs]
        yield
        u = [_bdot(tinv[p], on_s[p][:two_c] + on_v[p][:two_c]) for p in pairs]
        yield
        o_st = [on_s[p][two_c:] + on_v[p][two_c:] + _bdot(w_rb[p], u[p]) for p in pairs]
        for p in pairs:
            uv = jnp.concatenate([bf(u[p]), v_st[p]], axis=0)
            st_ref[p] = pair(p_end, p) * st[p] + _bdot_tn(uv, end_st[p])
        yield
        o = jnp.concatenate([x[:CHUNK] + x[CHUNK:] for x in o_st], axis=1)
        mean = head_sum(o) * (1.0 / R_DH)
        oc = o - mean
        yield
        var = head_sum(oc * oc) * (1.0 / R_DH)
        y = oc * lax.rsqrt(var + R_LN_EPS) * lnw_ref[...] + lnb_ref[...] + bonus
        valid = (t * tt + r0 + row_i) >= PAD
        out_ref[pl.ds(r0, CHUNK), :] = jnp.where(valid, y * gg, 0.0).astype(BF16)

    def steady(c, carry):
        _run_stages(stage_b(c), stage_a(c + 1), steps=(1, A_STEPS_PER_B_STEP))
        return carry

    n_chunks = tt // CHUNK
    _run_stages(stage_a(0))
    lax.fori_loop(0, n_chunks - 1, steady, 0)
    _run_stages(stage_b(n_chunks - 1))


def _rwkv(z, vecs, mats, batch, lp):
    m = z.shape[0]
    tt = LONG_TILE if lp % LONG_TILE == 0 else SEQ_TILE
    nt = lp // tt
    row = lambda b, t: b * nt + t
    base = 4 * M_W
    two_c = 2 * CHUNK
    handoff = [pltpu.VMEM((R_PAIRS, 2 * two_c, LANES), BF16),
               pltpu.VMEM((R_PAIRS, two_c, LANES), BF16),
               pltpu.VMEM((R_PAIRS, 2 * two_c, LANES), BF16),
               pltpu.VMEM((R_PAIRS, two_c, two_c), BF16),
               pltpu.VMEM((R_PAIRS, 2 * two_c, LANES), F32),
               pltpu.VMEM((R_PAIRS, two_c, two_c), BF16),
               pltpu.VMEM((R_PAIRS, two_c, two_c), F32),
               pltpu.VMEM((1, R_W), F32),
               pltpu.VMEM((CHUNK, R_W), F32),
               pltpu.VMEM((CHUNK, R_W), F32)]
    in_specs = [
        pl.BlockSpec((tt, R_W), lambda b, t: (row(b, t), base // R_W)),
        pl.BlockSpec((tt, R_W), lambda b, t: (row(b, t), base // R_W + 1)),
        pl.BlockSpec((tt, R_W), lambda b, t: (row(b, t), base // R_W + 2)),
        pl.BlockSpec((tt, LANES), lambda b, t: (row(b, t), (base + 3 * R_W) // LANES)),
        pl.BlockSpec((tt, LANES), lambda b, t: (row(b, t), (base + 3 * R_W) // LANES + 1)),
    ]
    mu_r, mu_k, mu_v, mu_wa, mu_g, w0, a0, k_k, k_a, r_k, ln_w, ln_b = vecs
    w2, a2, g2 = mats
    args = [mu_r, mu_k, mu_v, mu_wa, mu_g, w0, w2, a0, a2, g2, k_k, k_a, r_k, ln_w, ln_b]
    in_specs += [_const_spec(a.shape) for a in args]
    return pl.pallas_call(
        _rwkv_kernel, grid=(batch, nt), in_specs=in_specs,
        out_specs=pl.BlockSpec((tt, R_W), lambda b, t: (row(b, t), 0)),
        out_shape=jax.ShapeDtypeStruct((m, R_W), BF16),
        scratch_shapes=[pltpu.VMEM((SUBLANES, R_W), F32), pltpu.VMEM((SUBLANES, R_W), F32),
                        pltpu.VMEM((SUBLANES, R_W), F32), pltpu.VMEM((SUBLANES, LANES), F32),
                        pltpu.VMEM((SUBLANES, LANES), F32),
                        pltpu.VMEM((R_PAIRS, LANES, LANES), F32)] + handoff,
        compiler_params=pltpu.CompilerParams(dimension_semantics=("arbitrary", "arbitrary"),
                                             vmem_limit_bytes=VMEM_LIMIT),
        name="rwkv7")(z, z, z, z, z, *args)


def _retention_kernel(q_ref, k_ref, gate_ref, v_ref, out_ref, st_ref):
    t = pl.program_id(1)
    c = q_ref.shape[0]

    @pl.when(t == 0)
    def _():
        st_ref[...] = jnp.zeros_like(st_ref)

    ri = lax.broadcasted_iota(jnp.int32, (c, c), 0)
    ci = lax.broadcasted_iota(jnp.int32, (c, c), 1)
    causal = jnp.where(ci <= ri, 1.0, 0.0).astype(BF16)
    steps = lax.broadcasted_iota(jnp.int32, (c, 1), 0).astype(F32) + 1.0

    for h in range(T_HEADS):
        log_gamma = float(np.log(1.0 - 2.0 ** (-5.0 - h)))
        ks = slice(h * T_DK, (h + 1) * T_DK)
        vs = slice(h * T_DV, (h + 1) * T_DV)
        q = (q_ref[:, ks] * jnp.exp(log_gamma * steps)).astype(BF16)
        k = (k_ref[:, ks] * (jnp.exp(-log_gamma * steps) * (T_DK ** -0.5))).astype(BF16)
        vb = v_ref[:, vs]
        st = st_ref[h]
        s = _bdot_nt(q, k).astype(BF16) * causal
        o = _bdot(s, vb) + _bdot(q, st)
        st_ref[h] = float(np.exp(log_gamma * c)) * (st + _bdot_tn(k, vb))

        mu = jnp.mean(o, axis=-1, keepdims=True)
        oc = o - mu
        var = jnp.mean(oc * oc, axis=-1, keepdims=True)
        out_ref[:, vs] = (oc * lax.rsqrt(var + NORM_EPS) * gate_ref[:, vs]).astype(BF16)


def _retention(z, v, batch, lp):
    m = z.shape[0]
    tt = SEQ_TILE
    nt = lp // tt
    row = lambda b, t: b * nt + t
    in_specs = [
        pl.BlockSpec((tt, D_MODEL), lambda b, t: (row(b, t), 0)),
        pl.BlockSpec((tt, D_MODEL), lambda b, t: (row(b, t), 1)),
        pl.BlockSpec((tt, T_WV), lambda b, t: (row(b, t), 1)),
        pl.BlockSpec((tt, T_WV), lambda b, t: (row(b, t), 0)),
    ]
    return pl.pallas_call(
        _retention_kernel, grid=(batch, nt), in_specs=in_specs,
        out_specs=pl.BlockSpec((tt, T_WV), lambda b, t: (row(b, t), 0)),
        out_shape=jax.ShapeDtypeStruct((m, T_WV), BF16),
        scratch_shapes=[pltpu.VMEM((T_HEADS, T_DK, T_DV), F32)],
        compiler_params=pltpu.CompilerParams(dimension_semantics=("arbitrary", "arbitrary"),
                                             vmem_limit_bytes=VMEM_LIMIT),
        name="retention")(z, z, z, v)


def _ffn_kernel(n_mix, last, *refs):
    h_ref = refs[0]
    a_refs = refs[1:1 + n_mix]
    wo_refs = refs[1 + n_mix:1 + 2 * n_mix]
    g_ref, wv_ref, wg_ref, cw_ref, cb_ref, wd_ref = refs[1 + 2 * n_mix:7 + 2 * n_mix]
    rest = refs[7 + 2 * n_mix:]
    if last:
        gf_ref, hm_ref = rest[:2]
        am_refs = rest[2:2 + n_mix]
        rest = rest[2 + n_mix:]
    o_ref, carry_ref, hn_ref, act_ref = rest
    tm = h_ref.shape[0]

    def mixer_out(x_ref, x_refs):
        mixed = x_ref[...]
        for a_ref, wo_ref in zip(x_refs, wo_refs):
            mixed = mixed + jnp.dot(a_ref[...], wo_ref[...], preferred_element_type=F32)
        return mixed

    @pl.when(pl.program_id(1 if last else 0) == 0)
    def _():
        if last:
            meta_hn = _rmsnorm_bf16(mixer_out(hm_ref, am_refs), g_ref[...])
            meta_gate = jnp.dot(meta_hn, wg_ref[...], preferred_element_type=F32)
            carry_ref[...] = meta_gate[N_META - SUBLANES:, :]
        else:
            carry_ref[...] = jnp.zeros_like(carry_ref)

    mixed = mixer_out(h_ref, a_refs)
    hn_ref[...] = _rmsnorm_bf16(mixed, g_ref[...])
    row = lax.broadcasted_iota(jnp.int32, (SUBLANES, 1), 0)

    for c in range(N_FF_CHUNKS):
        cs = slice(c * FF_CHUNK, (c + 1) * FF_CHUNK)
        hn = hn_ref[...]
        val = jnp.dot(hn, wv_ref[:, cs], preferred_element_type=F32)
        gate = jnp.dot(hn, wg_ref[:, cs], preferred_element_type=F32)
        tail = carry_ref[:, cs]
        prev1 = pltpu.roll(gate, 1, axis=0)
        prev2 = pltpu.roll(gate, 2, axis=0)
        head1 = jnp.where(row == 0, tail[SUBLANES - 1:SUBLANES, :], prev1[:SUBLANES])
        head2 = jnp.where(row == 0, tail[SUBLANES - 2:SUBLANES - 1, :], prev2[:SUBLANES])
        head2 = jnp.where(row == 1, tail[SUBLANES - 1:SUBLANES, :], head2)
        prev1 = jnp.concatenate([head1, prev1[SUBLANES:]], axis=0)
        prev2 = jnp.concatenate([head2, prev2[SUBLANES:]], axis=0)
        carry_ref[:, cs] = gate[tm - SUBLANES:tm, :]
        conv = (cw_ref[2:3, cs] * gate + cw_ref[1:2, cs] * prev1 + cw_ref[0:1, cs] * prev2
                + cb_ref[:, cs])
        act_ref[:, cs] = (conv * _sigmoid(conv) * val).astype(BF16)
    out = mixed + jnp.dot(act_ref[...], wd_ref[...], preferred_element_type=F32)
    if last:
        ms = jnp.mean(out * out, axis=-1, keepdims=True)
        o_ref[0] = out * lax.rsqrt(ms + NORM_EPS) * gf_ref[...]
    else:
        o_ref[...] = out


def _mix_ffn(h, acts, w_outs, g, wv, wg, cw, cb, wd):
    m, d = h.shape
    tm = ROW_TILE
    in_specs = [pl.BlockSpec((tm, d), lambda i: (i, 0))]
    in_specs += [pl.BlockSpec((tm, a.shape[1]), lambda i: (i, 0)) for a in acts]
    in_specs += [_const_spec(a.shape) for a in (*w_outs, g, wv, wg, cw, cb, wd)]
    return pl.pallas_call(
        functools.partial(_ffn_kernel, len(acts), False), grid=(m // tm,), in_specs=in_specs,
        out_specs=pl.BlockSpec((tm, d), lambda i: (i, 0)),
        out_shape=jax.ShapeDtypeStruct((m, d), F32),
        scratch_shapes=[pltpu.VMEM((SUBLANES, D_FF), F32),
                        pltpu.VMEM((tm, d), BF16),
                        pltpu.VMEM((tm, D_FF), BF16)],
        compiler_params=pltpu.CompilerParams(dimension_semantics=("arbitrary",),
                                             vmem_limit_bytes=VMEM_LIMIT),
        name="mix_ffn")(h, *acts, *w_outs, g, wv, wg, cw, cb, wd)


def _last_mix_ffn(h, acts, w_outs, g, wv, wg, cw, cb, wd, g_final, batch, lp):
    d = h.shape[1]
    seq = lp - FRONT
    tm = ROW_TILE if seq % ROW_TILE == 0 else FRONT
    tile = lambda cols: pl.BlockSpec(
        (pl.Element(tm), pl.Element(cols)),
        lambda b, j: (pl.multiple_of(b * lp + FRONT + j * tm, FRONT), 0))
    meta = lambda cols: pl.BlockSpec(
        (pl.Element(N_META), pl.Element(cols)),
        lambda b, j: (pl.multiple_of(b * lp + PAD, N_META), 0))
    in_specs = [tile(d)] + [tile(a.shape[1]) for a in acts]
    in_specs += [_const_spec(a.shape) for a in (*w_outs, g, wv, wg, cw, cb, wd, g_final)]
    in_specs += [meta(d)] + [meta(a.shape[1]) for a in acts]
    return pl.pallas_call(
        functools.partial(_ffn_kernel, len(acts), True), grid=(batch, seq // tm),
        in_specs=in_specs,
        out_specs=pl.BlockSpec((1, tm, d), lambda b, j: (b, j, 0)),
        out_shape=jax.ShapeDtypeStruct((batch, seq, d), F32),
        scratch_shapes=[pltpu.VMEM((SUBLANES, D_FF), F32),
                        pltpu.VMEM((tm, d), BF16),
                        pltpu.VMEM((tm, D_FF), BF16)],
        compiler_params=pltpu.CompilerParams(dimension_semantics=("arbitrary", "arbitrary"),
                                             vmem_limit_bytes=VMEM_LIMIT),
        name="last_mix_ffn")(h, *acts, *w_outs, g, wv, wg, cw, cb, wd, g_final, h, *acts)


def _ffn_weights(w_up, conv_w, conv_b, w_down):
    wv = w_up[:, :D_FF].astype(BF16)
    wg = w_up[:, D_FF:].astype(BF16)
    cw = jnp.pad(conv_w, ((0, SUBLANES - conv_w.shape[0]), (0, 0)))
    return wv, wg, cw, conv_b.reshape(1, D_FF), w_down.astype(BF16)


def kernel(x, meta_tokens, norm_mix, norm_ffn, norm_final, e_w_in, e_w_out, m_b_i, m_b_f, m_norm,
           r_mu, r_w0, r_w2, r_a0, r_a2, r_g2, r_k_k, r_k_a, r_r_k, r_ln_w, r_ln_b, o_w_in, o_w_out,
           f_w_up, f_conv_w, f_conv_b, f_w_down):
    batch, seq, d = x.shape
    lp = seq + FRONT
    assert d == D_MODEL and lp % SEQ_TILE == 0 and (batch * lp) % ROW_TILE == 0
    m = batch * lp
    row = lambda v: v.reshape(1, -1).astype(F32)

    meta = jnp.broadcast_to(meta_tokens[None].astype(x.dtype), (batch, N_META, d))
    h = jnp.concatenate([jnp.zeros((batch, PAD, d), x.dtype), meta, x], axis=1).reshape(m, d)

    w_in = e_w_in[0]
    n_m = 4 * M_W
    gates_w = w_in[:, n_m:n_m + 2 * M_HEADS]
    w0 = jnp.concatenate([w_in[:, :n_m], w_in[:, n_m + 2 * M_HEADS:], gates_w,
                          jnp.zeros((d, LANES - 2 * M_HEADS), F32)], axis=1).astype(BF16)
    z, gt_chunks = _norm_proj(h, row(norm_mix[0]), w0, gates_w.T.astype(BF16))
    gate_bias = jnp.concatenate([m_b_i[0], m_b_f[0]])
    bias_col = jnp.pad(gate_bias, (0, LANES - 2 * M_HEADS)).reshape(1, LANES)
    bias_row = jnp.broadcast_to(gate_bias[:, None], (SUBLANES, M_CHUNK))
    mix_m = _mlstm(z, gt_chunks, bias_col, bias_row, row(m_norm[0]), batch, lp)

    mu = r_mu[0]
    rank_wa = r_w2.shape[1] + r_a2.shape[1]
    vecs = [row(mu[:R_W]), row(mu[R_W:2 * R_W]), row(mu[2 * R_W:3 * R_W]),
            row(mu[3 * R_W:3 * R_W + rank_wa]), row(mu[3 * R_W + rank_wa:]),
            row(r_w0[0]), row(r_a0[0]), row(r_k_k[0]), row(r_k_a[0]), row(r_r_k[0]),
            row(r_ln_w[0]), row(r_ln_b[0])]
    w2 = jnp.concatenate([r_w2[0], jnp.zeros_like(r_a2[0])], axis=0).astype(BF16)
    a2 = jnp.concatenate([jnp.zeros_like(r_w2[0]), r_a2[0]], axis=0).astype(BF16)
    mix_r = _rwkv(z, vecs, [w2, a2, r_g2[0].astype(BF16)], batch, lp)

    w_out = e_w_out[0].astype(BF16)
    h = _mix_ffn(h, [mix_m, mix_r], [w_out[:M_W], w_out[M_W:]], row(norm_ffn[0]),
                 *_ffn_weights(f_w_up[0], f_conv_w[0], f_conv_b[0], f_w_down[0]))

    w_in = o_w_in[0]
    perm = np.concatenate([np.arange(0, T_DK, 2), np.arange(1, T_DK, 2)])
    qk_cols = np.concatenate([hh * T_DK + perm for hh in range(T_HEADS)])
    w1 = jnp.concatenate([w_in[:, qk_cols], w_in[:, D_MODEL + qk_cols],
                          w_in[:, 2 * D_MODEL + T_WV:], w_in[:, 2 * D_MODEL:2 * D_MODEL + T_WV]],
                         axis=1).astype(BF16)
    inv = 1.0 / (ROPE_BASE ** jnp.linspace(0.0, 1.0, T_DK // 2, dtype=F32))
    pos = jnp.arange(lp, dtype=F32) - PAD
    ang = pos[:, None] * inv[None, :]
    cos = jnp.tile(jnp.concatenate([jnp.cos(ang), jnp.cos(ang)], axis=1), (batch, 1))
    sin = jnp.tile(jnp.concatenate([-jnp.sin(ang), jnp.sin(ang)], axis=1), (batch, 1))
    z, v = _norm_proj(h, row(norm_mix[1]), w1, rotary=(cos, sin))
    o = _retention(z, v, batch, lp)
    return _last_mix_ffn(h, [o], [o_w_out[0].astype(BF16)], row(norm_ffn[1]),
                         *_ffn_weights(f_w_up[1], f_conv_w[1], f_conv_b[1], f_w_down[1]),
                         row(norm_final), batch, lp)
```

```python
import functools

import numpy as np
import jax
import jax.numpy as jnp
from jax import lax
from jax.experimental import pallas as pl
from jax.experimental.pallas import tpu as pltpu

F32 = jnp.float32
BF16 = jnp.bfloat16

D_MODEL = 1024
N_META = 16
NORM_EPS = 1e-6
M_HEADS = 4
M_DH = 128
M_W = M_HEADS * M_DH
GATE_CAP = 15.0
R_DH = 64
R_HEADS = 8
R_W = R_HEADS * R_DH
R_PAIRS = R_HEADS // 2
R_LN_EPS = 64e-5
T_HEADS = 4
T_DK = 256
T_DV = 512
T_WV = T_HEADS * T_DV
ROPE_BASE = 10000.0
D_FF = 2816
FF_CHUNK = 256
N_FF_CHUNKS = D_FF // FF_CHUNK

LANES = 128
SUBLANES = 8
FRONT = 128
PAD = FRONT - N_META
CHUNK = 64
M_CHUNK = 128
ROW_TILE = 512
SEQ_TILE = 384
LONG_TILE = 1408
RWKV_TILE = 2112
DOUBLINGS_AHEAD = 1
A_STEPS_PER_B_STEP = 4
VMEM_LIMIT = 56 * 1024 * 1024


def _const_spec(shape):
    nd = len(shape)
    return pl.BlockSpec(shape, lambda *_: (0,) * nd, pipeline_mode=pl.Buffered(1))


def _sigmoid(x):
    return 1.0 / (1.0 + jnp.exp(-x))


def _softplus(x):
    return jnp.maximum(x, 0.0) + jnp.log1p(jnp.exp(-jnp.abs(x)))


def _bdot(a, b):
    return jnp.dot(a.astype(BF16), b.astype(BF16), preferred_element_type=F32)


def _bdot_nt(a, b):
    return lax.dot_general(a.astype(BF16), b.astype(BF16), (((1,), (1,)), ((), ())),
                           preferred_element_type=F32)


def _bdot_tn(a, b):
    return lax.dot_general(a.astype(BF16), b.astype(BF16), (((0,), (0,)), ((), ())),
                           preferred_element_type=F32)


def _split3(x):
    hi = x.astype(BF16)
    rest = x - hi.astype(F32)
    mid = rest.astype(BF16)
    return hi, mid, (rest - mid.astype(F32)).astype(BF16)


def _select_dot(sel, x):
    hi, mid, low = _split3(x)
    dot = lambda y: jnp.dot(sel, y, preferred_element_type=F32)
    return dot(hi) + dot(mid) + dot(low)


def _dot_select(x, sel):
    hi, mid, low = _split3(x)
    dot = lambda y: jnp.dot(y, sel, preferred_element_type=F32)
    return dot(hi) + dot(mid) + dot(low)


def _run_stages(*stages, steps=None):
    steps = steps or (1,) * len(stages)
    live = list(zip(stages, steps))
    done = object()
    while live:
        live = [(s, n) for s, n in live if all(next(s, done) is not done for _ in range(n))]


def _rmsnorm_bf16(x, g):
    ms = jnp.mean(x * x, axis=-1, keepdims=True)
    return (x * lax.rsqrt(ms + NORM_EPS) * g).astype(BF16)


def _col_chunks(n, width=512):
    return [(c, min(width, n - c)) for c in range(0, n, width)]


def _norm_proj_kernel(h_ref, g_ref, w_ref, o_ref, hn_ref):
    hn_ref[...] = _rmsnorm_bf16(h_ref[...], g_ref[...])
    for c0, cw in _col_chunks(w_ref.shape[1]):
        o_ref[:, c0:c0 + cw] = jnp.dot(hn_ref[...], w_ref[:, c0:c0 + cw],
                                       preferred_element_type=F32)


def _norm_proj_gates_kernel(h_ref, g_ref, w_ref, wgt_ref, o_ref, gt_ref, hn_ref):
    _norm_proj_kernel(h_ref, g_ref, w_ref, o_ref, hn_ref)
    for c in range(gt_ref.shape[0]):
        gt_ref[c] = lax.dot_general(wgt_ref[...], hn_ref[c * M_CHUNK:(c + 1) * M_CHUNK, :],
                                    (((1,), (1,)), ((), ())), preferred_element_type=F32)


def _norm_proj_retention_kernel(h_ref, g_ref, w_ref, cos_ref, sin_ref, o_ref, ob_ref, hn_ref):
    hn_ref[...] = _rmsnorm_bf16(h_ref[...], g_ref[...])
    n_f32 = o_ref.shape[1]
    half = T_DK // 2
    for c0, cw in _col_chunks(w_ref.shape[1], 2 * T_DK):
        z = jnp.dot(hn_ref[...], w_ref[:, c0:c0 + cw], preferred_element_type=F32)
        if c0 < 2 * D_MODEL:
            cos = cos_ref[...]
            sin = sin_ref[...]
            for j in range(0, cw, T_DK):
                x = z[:, j:j + T_DK]
                swapped = jnp.concatenate([x[:, half:], x[:, :half]], axis=1)
                o_ref[:, c0 + j:c0 + j + T_DK] = x * cos + swapped * sin
        elif c0 < n_f32:
            o_ref[:, c0:c0 + cw] = z * _sigmoid(z)
        else:
            ob_ref[:, c0 - n_f32:c0 - n_f32 + cw] = z.astype(BF16)


def _norm_proj(h, g, w, wgt=None, rotary=None):
    m, d = h.shape
    n = w.shape[1]
    tm = ROW_TILE
    in_specs = [pl.BlockSpec((tm, d), lambda i: (i, 0)), _const_spec((1, d)), _const_spec((d, n))]
    params = pltpu.CompilerParams(dimension_semantics=("arbitrary",), vmem_limit_bytes=VMEM_LIMIT)
    if wgt is None:
        n_f32 = n - T_WV
        table = pl.BlockSpec((tm, T_DK), lambda i: (i, 0))
        return pl.pallas_call(
            _norm_proj_retention_kernel, grid=(m // tm,), in_specs=in_specs + [table, table],
            out_specs=[pl.BlockSpec((tm, n_f32), lambda i: (i, 0)),
                       pl.BlockSpec((tm, T_WV), lambda i: (i, 0))],
            out_shape=[jax.ShapeDtypeStruct((m, n_f32), F32),
                       jax.ShapeDtypeStruct((m, T_WV), BF16)],
            scratch_shapes=[pltpu.VMEM((tm, d), BF16)],
            compiler_params=params, name="norm_proj")(h, g, w, *rotary)
    return pl.pallas_call(
        _norm_proj_gates_kernel, grid=(m // tm,),
        in_specs=in_specs + [_const_spec(wgt.shape)],
        out_specs=[pl.BlockSpec((tm, n), lambda i: (i, 0)),
                   pl.BlockSpec((tm // M_CHUNK, SUBLANES, M_CHUNK), lambda i: (i, 0, 0))],
        out_shape=[jax.ShapeDtypeStruct((m, n), F32),
                   jax.ShapeDtypeStruct((m // M_CHUNK, SUBLANES, M_CHUNK), F32)],
        scratch_shapes=[pltpu.VMEM((tm, d), BF16)],
        compiler_params=params, name="norm_proj_gates")(h, g, w, wgt)


def _gate_act(z, is_input_gate):
    capped = GATE_CAP * jnp.tanh(z / GATE_CAP)
    return jnp.where(is_input_gate, capped, -_softplus(-capped))


def _mlstm_kernel(q_ref, k_ref, v_ref, og_ref, gc_ref, gr_ref, bc_ref, br_ref, mn_ref,
                  out_ref, c_ref, m_ref, h_sv, h_kv, h_q, h_maxd, h_bcum, h_row, h_gate):
    t = pl.program_id(1)
    tt = q_ref.shape[0]

    @pl.when(t == 0)
    def _():
        c_ref[...] = jnp.zeros_like(c_ref)
        m_ref[...] = jnp.zeros_like(m_ref)

    cs = M_CHUNK
    ri = lax.broadcasted_iota(jnp.int32, (cs, cs), 0)
    ci = lax.broadcasted_iota(jnp.int32, (cs, cs), 1)
    causal = ci <= ri
    lower = jnp.where(causal, 1.0, 0.0).astype(BF16)
    upper = jnp.where(ri <= ci, 1.0, 0.0).astype(BF16)
    n_blocks = 2 * M_HEADS
    sel_r = lax.broadcasted_iota(jnp.int32, (LANES, n_blocks * LANES), 0)
    sel_c = lax.broadcasted_iota(jnp.int32, (LANES, n_blocks * LANES), 1)
    spread = jnp.where(sel_r == sel_c // LANES, 1.0, 0.0).astype(BF16)
    ones_blk = jnp.ones((cs, LANES), BF16)
    col_lane = lax.broadcasted_iota(jnp.int32, (1, LANES), 1)
    col_row = lax.broadcasted_iota(jnp.int32, (cs, 1), 0)
    row_sub = lax.broadcasted_iota(jnp.int32, (SUBLANES, 1), 0)
    row_lane = lax.broadcasted_iota(jnp.int32, (1, cs), 1)
    scale = M_DH ** -0.5
    heads = range(M_HEADS)
    hs = [slice(h * M_DH, (h + 1) * M_DH) for h in heads]
    blk = lambda x, j: x[:, j * LANES:(j + 1) * LANES]
    row0 = lambda c: c * cs if isinstance(c, int) else pl.multiple_of(c * cs, cs)
    finite = lambda x: jnp.where(x == -jnp.inf, 0.0, x)

    def stage_a(c):
        r0 = row0(c)
        pos0 = t * tt + r0
        gact = _gate_act(gc_ref[pl.ds(r0, cs), :] + bc_ref[...], col_lane < M_HEADS)
        valid = pos0 + col_row >= PAD
        bcum = _select_dot(lower, jnp.where(valid, gact, 0.0))
        spread_g = _dot_select(jnp.where(col_lane < M_HEADS, gact, bcum), spread)
        li_b = [jnp.where(valid, blk(spread_g, h), -jnp.inf) for h in heads]
        bcum_b = [blk(spread_g, M_HEADS + h) for h in heads]
        gr = _gate_act(gr_ref[c] + br_ref[...], row_sub < M_HEADS)
        gr = jnp.where(pos0 + row_lane >= PAD, gr, jnp.where(row_sub < M_HEADS, -jnp.inf, 0.0))
        bcum_r = _dot_select(jnp.where(row_sub < M_HEADS, 0.0, gr), upper)
        yield
        qb = [q_ref[pl.ds(r0, cs), hs[h]].astype(BF16) for h in heads]
        k = [k_ref[pl.ds(r0, cs), hs[h]] * scale for h in heads]
        vb = [jnp.concatenate([v_ref[pl.ds(r0, cs), hs[h]].astype(BF16), ones_blk], axis=1)
              for h in heads]
        qk = [_bdot_nt(qb[h], k[h]) for h in heads]
        yield
        brow = [bcum_r[M_HEADS + h:M_HEADS + h + 1, :] for h in heads]
        lirow = [gr[h:h + 1, :] for h in heads]
        g_b = [bcum_b[h][cs - 1:cs, :] for h in heads]
        dmat = [jnp.where(causal, bcum_b[h][:, :cs] - brow[h] + lirow[h], -jnp.inf)
                for h in heads]
        maxd = [jnp.max(dmat[h], axis=-1, keepdims=True) for h in heads]
        a_row = [g_b[h][:, :1] - brow[h] + lirow[h] for h in heads]
        amax = [jnp.max(a_row[h], axis=-1, keepdims=True) for h in heads]
        yield
        sp = [qk[h] * jnp.exp(dmat[h] - finite(maxd[h])) for h in heads]
        kwp = [k[h] * jnp.exp(g_b[h] - bcum_b[h] + li_b[h] - finite(amax[h])) for h in heads]
        svp = [_bdot(sp[h], vb[h]) for h in heads]
        yield
        kvp = [_bdot_tn(kwp[h], vb[h]) for h in heads]
        yield
        for h in heads:
            og = og_ref[pl.ds(r0, cs), hs[h]]
            h_sv[h] = svp[h]
            h_kv[h] = kvp[h]
            h_q[h] = qb[h]
            h_maxd[h] = jnp.broadcast_to(maxd[h], (cs, LANES))
            h_bcum[h] = bcum_b[h]
            h_row[h] = jnp.concatenate([jnp.broadcast_to(amax[h], (1, LANES)), g_b[h]], axis=0)
            h_gate[h] = mn_ref[:, hs[h]] * _sigmoid(og)

    def stage_b(c):
        r0 = row0(c)
        svp = [h_sv[h] for h in heads]
        kvp = [h_kv[h] for h in heads]
        qb = [h_q[h] for h in heads]
        maxd = [h_maxd[h] for h in heads]
        bcum_b = [h_bcum[h] for h in heads]
        amax = [h_row[h][0:1, :] for h in heads]
        g_b = [h_row[h][1:2, :] for h in heads]
        gate = [h_gate[h] for h in heads]
        c_st = [c_ref[h] for h in heads]
        m_st = [m_ref[h] for h in heads]
        qcn = [_bdot(qb[h], c_st[h]) for h in heads]
        inter = [bcum_b[h] + m_st[h] for h in heads]
        m_row = [jnp.maximum(inter[h], maxd[h]) for h in heads]
        yield
        m_new = [jnp.maximum(g_b[h] + m_st[h], amax[h]) for h in heads]
        for h in heads:
            decay = jnp.exp(g_b[h] + m_st[h] - m_new[h])[:, :1]
            grow = jnp.exp(finite(amax[h]) - m_new[h])[:, :1]
            c_ref[h] = decay * c_st[h] + grow * kvp[h]
            m_ref[h] = m_new[h]
        yield
        f_intra = [jnp.exp(finite(maxd[h]) - m_row[h]) for h in heads]
        w_inter = [jnp.exp(inter[h] - m_row[h]) for h in heads]
        yield
        for h in heads:
            num = f_intra[h] * svp[h][:, :M_DH] + w_inter[h] * qcn[h][:, :M_DH]
            den = f_intra[h] * svp[h][:, M_DH:] + w_inter[h] * qcn[h][:, M_DH:]
            hc = num / jnp.maximum(jnp.abs(den), jnp.exp(-m_row[h]))
            hn = hc * lax.rsqrt(jnp.mean(hc * hc, axis=-1, keepdims=True) + NORM_EPS)
            out_ref[pl.ds(r0, cs), hs[h]] = (hn * gate[h]).astype(BF16)

    def steady(c, carry):
        _run_stages(stage_b(c), stage_a(c + 1))
        return carry

    n_chunks = tt // cs
    _run_stages(stage_a(0))
    lax.fori_loop(0, n_chunks - 1, steady, 0)
    _run_stages(stage_b(n_chunks - 1))


def _mlstm(z, gt_chunks, bias_col, bias_row, m_norm, batch, lp):
    m = z.shape[0]
    tt = LONG_TILE if lp % LONG_TILE == 0 else SEQ_TILE
    nt = lp // tt
    row = lambda b, t: b * nt + t
    gate_col_block = 4 * M_W // LANES + (3 * R_W + 256) // LANES
    handoff = [pltpu.VMEM((M_HEADS, M_CHUNK, 2 * M_DH), F32),
               pltpu.VMEM((M_HEADS, M_DH, 2 * M_DH), F32),
               pltpu.VMEM((M_HEADS, M_CHUNK, M_DH), BF16),
               pltpu.VMEM((M_HEADS, M_CHUNK, LANES), F32),
               pltpu.VMEM((M_HEADS, M_CHUNK, LANES), F32),
               pltpu.VMEM((M_HEADS, 2, LANES), F32),
               pltpu.VMEM((M_HEADS, M_CHUNK, M_DH), F32)]
    in_specs = [
        pl.BlockSpec((tt, M_W), lambda b, t: (row(b, t), 0)),
        pl.BlockSpec((tt, M_W), lambda b, t: (row(b, t), 1)),
        pl.BlockSpec((tt, M_W), lambda b, t: (row(b, t), 2)),
        pl.BlockSpec((tt, M_W), lambda b, t: (row(b, t), 3)),
        pl.BlockSpec((tt, LANES), lambda b, t: (row(b, t), gate_col_block)),
        pl.BlockSpec((tt // M_CHUNK, SUBLANES, M_CHUNK), lambda b, t: (row(b, t), 0, 0)),
        _const_spec((1, LANES)), _const_spec((SUBLANES, M_CHUNK)), _const_spec((1, M_W)),
    ]
    return pl.pallas_call(
        _mlstm_kernel, grid=(batch, nt), in_specs=in_specs,
        out_specs=pl.BlockSpec((tt, M_W), lambda b, t: (row(b, t), 0)),
        out_shape=jax.ShapeDtypeStruct((m, M_W), BF16),
        scratch_shapes=[pltpu.VMEM((M_HEADS, M_DH, 2 * M_DH), F32),
                        pltpu.VMEM((M_HEADS, 1, LANES), F32)] + handoff,
        compiler_params=pltpu.CompilerParams(dimension_semantics=("arbitrary", "arbitrary"),
                                             vmem_limit_bytes=VMEM_LIMIT),
        name="mlstm")(z, z, z, z, z, gt_chunks, bias_col, bias_row, m_norm)


def _rwkv_kernel(r_ref, k_ref, v_ref, wa_ref, g_ref, mur_ref, muk_ref, muv_ref, muwa_ref, mug_ref,
                 w0_ref, w2_ref, a0_ref, a2_ref, g2_ref, kk_ref, ka_ref, rk_ref, lnw_ref, lnb_ref,
                 out_ref, cr_ref, ck_ref, cv_ref, cwa_ref, cg_ref, st_ref,
                 h_left, h_vst, h_end, h_wrb, h_onv, h_apow, h_tinv, h_pend, h_bonus, h_gg):
    t = pl.program_id(1)
    tt = r_ref.shape[0]
    two_c = 2 * CHUNK

    @pl.when(t == 0)
    def _():
        for ref in (cr_ref, ck_ref, cv_ref, cwa_ref, cg_ref, st_ref):
            ref[...] = jnp.zeros_like(ref)

    row_i = lax.broadcasted_iota(jnp.int32, (CHUNK, 1), 0)
    row_8 = lax.broadcasted_iota(jnp.int32, (SUBLANES, 1), 0)
    lo = lax.broadcasted_iota(jnp.int32, (1, LANES), 1) < R_DH
    ri = lax.broadcasted_iota(jnp.int32, (CHUNK, CHUNK), 0)
    ci = lax.broadcasted_iota(jnp.int32, (CHUNK, CHUNK), 1)
    lower = (ci <= ri).astype(BF16)
    r4 = lax.broadcasted_iota(jnp.int32, (4 * CHUNK, 4 * CHUNK), 0)
    c4 = lax.broadcasted_iota(jnp.int32, (4 * CHUNK, 4 * CHUNK), 1)
    same_head = ((r4 // CHUNK) % 2) == ((c4 // CHUNK) % 2)
    below = (c4 % CHUNK < r4 % CHUNK) | ((r4 >= two_c) & (c4 % CHUNK == r4 % CHUNK))
    keep = jnp.where(same_head & below, 1.0, 0.0).astype(BF16)
    r2 = lax.broadcasted_iota(jnp.int32, (two_c, two_c), 0)
    c2 = lax.broadcasted_iota(jnp.int32, (two_c, two_c), 1)
    eye2 = (r2 == c2).astype(F32)
    decay_scale = float(np.exp(-0.5))

    def head_sum(x):
        parts = []
        for p in range(R_PAIRS):
            xp = x[:, p * LANES:(p + 1) * LANES]
            s0 = jnp.sum(jnp.where(lo, xp, 0.0), axis=-1, keepdims=True)
            s1 = jnp.sum(jnp.where(lo, 0.0, xp), axis=-1, keepdims=True)
            parts.append(jnp.where(lo, s0, s1))
        return jnp.concatenate(parts, axis=1)

    def stack_heads(x):
        return jnp.concatenate([jnp.where(lo, x, 0.0), jnp.where(lo, 0.0, x)], axis=0)

    pairs = range(R_PAIRS)
    pair = lambda x, p: x[:, p * LANES:(p + 1) * LANES]
    bf = lambda x: x.astype(BF16)
    row0 = lambda c: c * CHUNK if isinstance(c, int) else pl.multiple_of(c * CHUNK, CHUNK)

    def double(apow, tinv):
        both = [_bdot(apow[p], jnp.concatenate([bf(apow[p]), bf(tinv[p])], axis=1)) for p in pairs]
        return [x[:, :two_c] for x in both], [tinv[p] + both[p][:, two_c:] for p in pairs]

    def stage_a(c):
        r0 = row0(c)

        def shift_mix(x_ref, c_ref, mu_ref):
            x = x_ref[pl.ds(r0, CHUNK), :]
            prev = pltpu.roll(x, 1, axis=0)
            head = jnp.where(row_8 == 0, c_ref[SUBLANES - 1:SUBLANES, :], prev[:SUBLANES])
            prev = jnp.concatenate([head, prev[SUBLANES:]], axis=0)
            c_ref[...] = x[CHUNK - SUBLANES:CHUNK, :]
            return x + (prev - x) * mu_ref[...]

        xwa = shift_mix(wa_ref, cwa_ref, muwa_ref)
        y_w = _bdot(jnp.tanh(xwa), w2_ref[...])
        y_a = _bdot(xwa, a2_ref[...])
        yield
        xg = shift_mix(g_ref, cg_ref, mug_ref)
        gg = _bdot(_sigmoid(xg), g2_ref[...])
        yield
        rr = shift_mix(r_ref, cr_ref, mur_ref)
        yield
        kr = shift_mix(k_ref, ck_ref, muk_ref)
        kk = kr * kk_ref[...]
        yield
        kk = kk * lax.rsqrt(jnp.maximum(head_sum(kk * kk), 1e-24))
        yield
        vr = shift_mix(v_ref, cv_ref, muv_ref)
        yield
        lw = -decay_scale * _sigmoid(w0_ref[...] + y_w)
        cum = _select_dot(lower, lw)
        yield
        aa = _sigmoid(a0_ref[...] + y_a)
        k2 = kr * (1.0 + (aa - 1.0) * ka_ref[...])
        yield
        bonus = head_sum(rr * k2 * rk_ref[...]) * vr
        yield
        e_pos = jnp.exp(cum)
        e_neg = jnp.exp(-cum)
        p_end = e_pos[CHUNK - 1:CHUNK, :]
        yield
        alpha = -kk * jnp.exp(cum - lw)
        rb = rr * e_pos
        yield
        beta = kk * aa * e_neg
        kt = k2 * e_neg
        yield
        beta_end = beta * p_end
        kt_end = kt * p_end
        yield
        left, prod = [], []
        for p in pairs:
            left.append(jnp.concatenate([stack_heads(pair(alpha, p)), stack_heads(pair(rb, p))],
                                        axis=0).astype(BF16))
            right = jnp.concatenate([pair(beta, p)] * 2 + [pair(kt, p)] * 2, axis=0)
            prod.append(_bdot_nt(left[p], right))
            yield
        v_st, end_st = [], []
        for p in pairs:
            v_st.append(stack_heads(pair(vr, p)).astype(BF16))
            end_st.append(jnp.concatenate([stack_heads(pair(beta_end, p)),
                                           stack_heads(pair(kt_end, p))], axis=0).astype(BF16))
            yield
        w_rb, tinv, apow, on_v = [], [], [], []
        for p in pairs:
            masked = prod[p].astype(BF16) * keep
            a_ab = masked[:two_c, :two_c]
            w_rb.append(masked[two_c:, :two_c])
            tinv.append(eye2 + a_ab.astype(F32))
            apow.append(_bdot(a_ab, a_ab))
            on_v.append(_bdot(masked[:, two_c:], v_st[p]))
            yield
        for _ in range(DOUBLINGS_AHEAD):
            apow, tinv = double(apow, tinv)
            yield
        for p in pairs:
            h_left[p] = left[p]
            h_vst[p] = v_st[p]
            h_end[p] = end_st[p]
            h_wrb[p] = w_rb[p]
            h_onv[p] = on_v[p]
            h_apow[p] = bf(apow[p])
            h_tinv[p] = tinv[p]
        h_pend[...] = p_end
        h_bonus[...] = bonus
        h_gg[...] = gg

    def stage_b(c):
        r0 = row0(c)
        left = [h_left[p] for p in pairs]
        v_st = [h_vst[p] for p in pairs]
        end_st = [h_end[p] for p in pairs]
        w_rb = [h_wrb[p] for p in pairs]
        on_v = [h_onv[p] for p in pairs]
        apow = [h_apow[p] for p in pairs]
        tinv = [h_tinv[p] for p in pairs]
        p_end = h_pend[...]
        bonus = h_bonus[...]
        gg = h_gg[...]
        st = [st_ref[p] for p in pairs]
        for _ in range(4 - DOUBLINGS_AHEAD):
            apow, tinv = double(apow, tinv)
            yield
        on_s = [_bdot_nt(left[p], st[p]) for p in pairs]
        tinv = [tinv[p] + _bdot(apow[p], tinv[p]) for p in pairs]
        yield
        u = [_bdot(tinv[p], on_s[p][:two_c] + on_v[p][:two_c]) for p in pairs]
        yield
        o_st = [on_s[p][two_c:] + on_v[p][two_c:] + _bdot(w_rb[p], u[p]) for p in pairs]
        for p in pairs:
            uv = jnp.concatenate([bf(u[p]), v_st[p]], axis=0)
            st_ref[p] = pair(p_end, p) * st[p] + _bdot_tn(uv, end_st[p])
        yield
        o = jnp.concatenate([x[:CHUNK] + x[CHUNK:] for x in o_st], axis=1)
        mean = head_sum(o) * (1.0 / R_DH)
        oc = o - mean
        yield
        var = head_sum(oc * oc) * (1.0 / R_DH)
        y = oc * lax.rsqrt(var + R_LN_EPS) * lnw_ref[...] + lnb_ref[...] + bonus
        valid = (t * tt + r0 + row_i) >= PAD
        out_ref[pl.ds(r0, CHUNK), :] = jnp.where(valid, y * gg, 0.0).astype(BF16)

    def steady(c, carry):
        _run_stages(stage_b(c), stage_a(c + 1), steps=(1, A_STEPS_PER_B_STEP))
        return carry

    n_chunks = tt // CHUNK
    _run_stages(stage_a(0))
    lax.fori_loop(0, n_chunks - 1, steady, 0)
    _run_stages(stage_b(n_chunks - 1))


def _rwkv(z, vecs, mats, batch, lp):
    m = z.shape[0]
    tt = RWKV_TILE if lp % RWKV_TILE == 0 else SEQ_TILE
    nt = lp // tt
    row = lambda b, t: b * nt + t
    base = 4 * M_W
    two_c = 2 * CHUNK
    handoff = [pltpu.VMEM((R_PAIRS, 2 * two_c, LANES), BF16),
               pltpu.VMEM((R_PAIRS, two_c, LANES), BF16),
               pltpu.VMEM((R_PAIRS, 2 * two_c, LANES), BF16),
               pltpu.VMEM((R_PAIRS, two_c, two_c), BF16),
               pltpu.VMEM((R_PAIRS, 2 * two_c, LANES), F32),
               pltpu.VMEM((R_PAIRS, two_c, two_c), BF16),
               pltpu.VMEM((R_PAIRS, two_c, two_c), F32),
               pltpu.VMEM((1, R_W), F32),
               pltpu.VMEM((CHUNK, R_W), F32),
               pltpu.VMEM((CHUNK, R_W), F32)]
    in_specs = [
        pl.BlockSpec((tt, R_W), lambda b, t: (row(b, t), base // R_W)),
        pl.BlockSpec((tt, R_W), lambda b, t: (row(b, t), base // R_W + 1)),
        pl.BlockSpec((tt, R_W), lambda b, t: (row(b, t), base // R_W + 2)),
        pl.BlockSpec((tt, LANES), lambda b, t: (row(b, t), (base + 3 * R_W) // LANES)),
        pl.BlockSpec((tt, LANES), lambda b, t: (row(b, t), (base + 3 * R_W) // LANES + 1)),
    ]
    mu_r, mu_k, mu_v, mu_wa, mu_g, w0, a0, k_k, k_a, r_k, ln_w, ln_b = vecs
    w2, a2, g2 = mats
    args = [mu_r, mu_k, mu_v, mu_wa, mu_g, w0, w2, a0, a2, g2, k_k, k_a, r_k, ln_w, ln_b]
    in_specs += [_const_spec(a.shape) for a in args]
    return pl.pallas_call(
        _rwkv_kernel, grid=(batch, nt), in_specs=in_specs,
        out_specs=pl.BlockSpec((tt, R_W), lambda b, t: (row(b, t), 0)),
        out_shape=jax.ShapeDtypeStruct((m, R_W), BF16),
        scratch_shapes=[pltpu.VMEM((SUBLANES, R_W), F32), pltpu.VMEM((SUBLANES, R_W), F32),
                        pltpu.VMEM((SUBLANES, R_W), F32), pltpu.VMEM((SUBLANES, LANES), F32),
                        pltpu.VMEM((SUBLANES, LANES), F32),
                        pltpu.VMEM((R_PAIRS, LANES, LANES), F32)] + handoff,
        compiler_params=pltpu.CompilerParams(dimension_semantics=("arbitrary", "arbitrary"),
                                             vmem_limit_bytes=VMEM_LIMIT),
        name="rwkv7")(z, z, z, z, z, *args)


def _retention_kernel(q_ref, k_ref, gate_ref, v_ref, out_ref, st_ref):
    t = pl.program_id(1)
    c = q_ref.shape[0]

    @pl.when(t == 0)
    def _():
        st_ref[...] = jnp.zeros_like(st_ref)

    ri = lax.broadcasted_iota(jnp.int32, (c, c), 0)
    ci = lax.broadcasted_iota(jnp.int32, (c, c), 1)
    causal = jnp.where(ci <= ri, 1.0, 0.0).astype(BF16)
    steps = lax.broadcasted_iota(jnp.int32, (c, 1), 0).astype(F32) + 1.0

    for h in range(T_HEADS):
        log_gamma = float(np.log(1.0 - 2.0 ** (-5.0 - h)))
        ks = slice(h * T_DK, (h + 1) * T_DK)
        vs = slice(h * T_DV, (h + 1) * T_DV)
        q = (q_ref[:, ks] * jnp.exp(log_gamma * steps)).astype(BF16)
        k = (k_ref[:, ks] * (jnp.exp(-log_gamma * steps) * (T_DK ** -0.5))).astype(BF16)
        vb = v_ref[:, vs]
        st = st_ref[h]
        s = _bdot_nt(q, k).astype(BF16) * causal
        o = _bdot(s, vb) + _bdot(q, st)
        st_ref[h] = float(np.exp(log_gamma * c)) * (st + _bdot_tn(k, vb))

        mu = jnp.mean(o, axis=-1, keepdims=True)
        oc = o - mu
        var = jnp.mean(oc * oc, axis=-1, keepdims=True)
        out_ref[:, vs] = (oc * lax.rsqrt(var + NORM_EPS) * gate_ref[:, vs]).astype(BF16)


def _retention(z, v, batch, lp):
    m = z.shape[0]
    tt = SEQ_TILE
    nt = lp // tt
    row = lambda b, t: b * nt + t
    in_specs = [
        pl.BlockSpec((tt, D_MODEL), lambda b, t: (row(b, t), 0)),
        pl.BlockSpec((tt, D_MODEL), lambda b, t: (row(b, t), 1)),
        pl.BlockSpec((tt, T_WV), lambda b, t: (row(b, t), 1)),
        pl.BlockSpec((tt, T_WV), lambda b, t: (row(b, t), 0)),
    ]
    return pl.pallas_call(
        _retention_kernel, grid=(batch, nt), in_specs=in_specs,
        out_specs=pl.BlockSpec((tt, T_WV), lambda b, t: (row(b, t), 0)),
        out_shape=jax.ShapeDtypeStruct((m, T_WV), BF16),
        scratch_shapes=[pltpu.VMEM((T_HEADS, T_DK, T_DV), F32)],
        compiler_params=pltpu.CompilerParams(dimension_semantics=("arbitrary", "arbitrary"),
                                             vmem_limit_bytes=VMEM_LIMIT),
        name="retention")(z, z, z, v)


def _ffn_kernel(n_mix, last, *refs):
    h_ref = refs[0]
    a_refs = refs[1:1 + n_mix]
    wo_refs = refs[1 + n_mix:1 + 2 * n_mix]
    g_ref, wv_ref, wg_ref, cw_ref, cb_ref, wd_ref = refs[1 + 2 * n_mix:7 + 2 * n_mix]
    rest = refs[7 + 2 * n_mix:]
    if last:
        gf_ref, hm_ref = rest[:2]
        am_refs = rest[2:2 + n_mix]
        rest = rest[2 + n_mix:]
    o_ref, carry_ref, hn_ref, act_ref = rest
    tm = h_ref.shape[0]

    def mixer_out(x_ref, x_refs):
        mixed = x_ref[...]
        for a_ref, wo_ref in zip(x_refs, wo_refs):
            mixed = mixed + jnp.dot(a_ref[...], wo_ref[...], preferred_element_type=F32)
        return mixed

    @pl.when(pl.program_id(1 if last else 0) == 0)
    def _():
        if last:
            meta_hn = _rmsnorm_bf16(mixer_out(hm_ref, am_refs), g_ref[...])
            meta_gate = jnp.dot(meta_hn, wg_ref[...], preferred_element_type=F32)
            carry_ref[...] = meta_gate[N_META - SUBLANES:, :]
        else:
            carry_ref[...] = jnp.zeros_like(carry_ref)

    mixed = mixer_out(h_ref, a_refs)
    hn_ref[...] = _rmsnorm_bf16(mixed, g_ref[...])
    row = lax.broadcasted_iota(jnp.int32, (SUBLANES, 1), 0)

    for c in range(N_FF_CHUNKS):
        cs = slice(c * FF_CHUNK, (c + 1) * FF_CHUNK)
        hn = hn_ref[...]
        val = jnp.dot(hn, wv_ref[:, cs], preferred_element_type=F32)
        gate = jnp.dot(hn, wg_ref[:, cs], preferred_element_type=F32)
        tail = carry_ref[:, cs]
        prev1 = pltpu.roll(gate, 1, axis=0)
        prev2 = pltpu.roll(gate, 2, axis=0)
        head1 = jnp.where(row == 0, tail[SUBLANES - 1:SUBLANES, :], prev1[:SUBLANES])
        head2 = jnp.where(row == 0, tail[SUBLANES - 2:SUBLANES - 1, :], prev2[:SUBLANES])
        head2 = jnp.where(row == 1, tail[SUBLANES - 1:SUBLANES, :], head2)
        prev1 = jnp.concatenate([head1, prev1[SUBLANES:]], axis=0)
        prev2 = jnp.concatenate([head2, prev2[SUBLANES:]], axis=0)
        carry_ref[:, cs] = gate[tm - SUBLANES:tm, :]
        conv = (cw_ref[2:3, cs] * gate + cw_ref[1:2, cs] * prev1 + cw_ref[0:1, cs] * prev2
                + cb_ref[:, cs])
        act_ref[:, cs] = (conv * _sigmoid(conv) * val).astype(BF16)
    out = mixed + jnp.dot(act_ref[...], wd_ref[...], preferred_element_type=F32)
    if last:
        ms = jnp.mean(out * out, axis=-1, keepdims=True)
        o_ref[0] = out * lax.rsqrt(ms + NORM_EPS) * gf_ref[...]
    else:
        o_ref[...] = out


def _mix_ffn(h, acts, w_outs, g, wv, wg, cw, cb, wd):
    m, d = h.shape
    tm = ROW_TILE
    in_specs = [pl.BlockSpec((tm, d), lambda i: (i, 0))]
    in_specs += [pl.BlockSpec((tm, a.shape[1]), lambda i: (i, 0)) for a in acts]
    in_specs += [_const_spec(a.shape) for a in (*w_outs, g, wv, wg, cw, cb, wd)]
    return pl.pallas_call(
        functools.partial(_ffn_kernel, len(acts), False), grid=(m // tm,), in_specs=in_specs,
        out_specs=pl.BlockSpec((tm, d), lambda i: (i, 0)),
        out_shape=jax.ShapeDtypeStruct((m, d), F32),
        scratch_shapes=[pltpu.VMEM((SUBLANES, D_FF), F32),
                        pltpu.VMEM((tm, d), BF16),
                        pltpu.VMEM((tm, D_FF), BF16)],
        compiler_params=pltpu.CompilerParams(dimension_semantics=("arbitrary",),
                                             vmem_limit_bytes=VMEM_LIMIT),
        name="mix_ffn")(h, *acts, *w_outs, g, wv, wg, cw, cb, wd)


def _last_mix_ffn(h, acts, w_outs, g, wv, wg, cw, cb, wd, g_final, batch, lp):
    d = h.shape[1]
    seq = lp - FRONT
    tm = ROW_TILE if seq % ROW_TILE == 0 else FRONT
    tile = lambda cols: pl.BlockSpec(
        (pl.Element(tm), pl.Element(cols)),
        lambda b, j: (pl.multiple_of(b * lp + FRONT + j * tm, FRONT), 0))
    meta = lambda cols: pl.BlockSpec(
        (pl.Element(N_META), pl.Element(cols)),
        lambda b, j: (pl.multiple_of(b * lp + PAD, N_META), 0))
    in_specs = [tile(d)] + [tile(a.shape[1]) for a in acts]
    in_specs += [_const_spec(a.shape) for a in (*w_outs, g, wv, wg, cw, cb, wd, g_final)]
    in_specs += [meta(d)] + [meta(a.shape[1]) for a in acts]
    return pl.pallas_call(
        functools.partial(_ffn_kernel, len(acts), True), grid=(batch, seq // tm),
        in_specs=in_specs,
        out_specs=pl.BlockSpec((1, tm, d), lambda b, j: (b, j, 0)),
        out_shape=jax.ShapeDtypeStruct((batch, seq, d), F32),
        scratch_shapes=[pltpu.VMEM((SUBLANES, D_FF), F32),
                        pltpu.VMEM((tm, d), BF16),
                        pltpu.VMEM((tm, D_FF), BF16)],
        compiler_params=pltpu.CompilerParams(dimension_semantics=("arbitrary", "arbitrary"),
                                             vmem_limit_bytes=VMEM_LIMIT),
        name="last_mix_ffn")(h, *acts, *w_outs, g, wv, wg, cw, cb, wd, g_final, h, *acts)


def _ffn_weights(w_up, conv_w, conv_b, w_down):
    wv = w_up[:, :D_FF].astype(BF16)
    wg = w_up[:, D_FF:].astype(BF16)
    cw = jnp.pad(conv_w, ((0, SUBLANES - conv_w.shape[0]), (0, 0)))
    return wv, wg, cw, conv_b.reshape(1, D_FF), w_down.astype(BF16)


def kernel(x, meta_tokens, norm_mix, norm_ffn, norm_final, e_w_in, e_w_out, m_b_i, m_b_f, m_norm,
           r_mu, r_w0, r_w2, r_a0, r_a2, r_g2, r_k_k, r_k_a, r_r_k, r_ln_w, r_ln_b, o_w_in, o_w_out,
           f_w_up, f_conv_w, f_conv_b, f_w_down):
    batch, seq, d = x.shape
    lp = seq + FRONT
    assert d == D_MODEL and lp % SEQ_TILE == 0 and (batch * lp) % ROW_TILE == 0
    m = batch * lp
    row = lambda v: v.reshape(1, -1).astype(F32)

    meta = jnp.broadcast_to(meta_tokens[None].astype(x.dtype), (batch, N_META, d))
    h = jnp.concatenate([jnp.zeros((batch, PAD, d), x.dtype), meta, x], axis=1).reshape(m, d)

    w_in = e_w_in[0]
    n_m = 4 * M_W
    gates_w = w_in[:, n_m:n_m + 2 * M_HEADS]
    w0 = jnp.concatenate([w_in[:, :n_m], w_in[:, n_m + 2 * M_HEADS:], gates_w,
                          jnp.zeros((d, LANES - 2 * M_HEADS), F32)], axis=1).astype(BF16)
    z, gt_chunks = _norm_proj(h, row(norm_mix[0]), w0, gates_w.T.astype(BF16))
    gate_bias = jnp.concatenate([m_b_i[0], m_b_f[0]])
    bias_col = jnp.pad(gate_bias, (0, LANES - 2 * M_HEADS)).reshape(1, LANES)
    bias_row = jnp.broadcast_to(gate_bias[:, None], (SUBLANES, M_CHUNK))
    mix_m = _mlstm(z, gt_chunks, bias_col, bias_row, row(m_norm[0]), batch, lp)

    mu = r_mu[0]
    rank_wa = r_w2.shape[1] + r_a2.shape[1]
    vecs = [row(mu[:R_W]), row(mu[R_W:2 * R_W]), row(mu[2 * R_W:3 * R_W]),
            row(mu[3 * R_W:3 * R_W + rank_wa]), row(mu[3 * R_W + rank_wa:]),
            row(r_w0[0]), row(r_a0[0]), row(r_k_k[0]), row(r_k_a[0]), row(r_r_k[0]),
            row(r_ln_w[0]), row(r_ln_b[0])]
    w2 = jnp.concatenate([r_w2[0], jnp.zeros_like(r_a2[0])], axis=0).astype(BF16)
    a2 = jnp.concatenate([jnp.zeros_like(r_w2[0]), r_a2[0]], axis=0).astype(BF16)
    mix_r = _rwkv(z, vecs, [w2, a2, r_g2[0].astype(BF16)], batch, lp)

    w_out = e_w_out[0].astype(BF16)
    h = _mix_ffn(h, [mix_m, mix_r], [w_out[:M_W], w_out[M_W:]], row(norm_ffn[0]),
                 *_ffn_weights(f_w_up[0], f_conv_w[0], f_conv_b[0], f_w_down[0]))

    w_in = o_w_in[0]
    perm = np.concatenate([np.arange(0, T_DK, 2), np.arange(1, T_DK, 2)])
    qk_cols = np.concatenate([hh * T_DK + perm for hh in range(T_HEADS)])
    w1 = jnp.concatenate([w_in[:, qk_cols], w_in[:, D_MODEL + qk_cols],
                          w_in[:, 2 * D_MODEL + T_WV:], w_in[:, 2 * D_MODEL:2 * D_MODEL + T_WV]],
                         axis=1).astype(BF16)
    inv = 1.0 / (ROPE_BASE ** jnp.linspace(0.0, 1.0, T_DK // 2, dtype=F32))
    pos = jnp.arange(lp, dtype=F32) - PAD
    ang = pos[:, None] * inv[None, :]
    cos = jnp.tile(jnp.concatenate([jnp.cos(ang), jnp.cos(ang)], axis=1), (batch, 1))
    sin = jnp.tile(jnp.concatenate([-jnp.sin(ang), jnp.sin(ang)], axis=1), (batch, 1))
    z, v = _norm_proj(h, row(norm_mix[1]), w1, rotary=(cos, sin))
    o = _retention(z, v, batch, lp)
    return _last_mix_ffn(h, [o], [o_w_out[0].astype(BF16)], row(norm_ffn[1]),
                         *_ffn_weights(f_w_up[1], f_conv_w[1], f_conv_b[1], f_w_down[1]),
                         row(norm_final), batch, lp)
```
